```python
import jax, jax.numpy as jnp
from jax import lax
import numpy as np

D_MODEL = 1024
BATCH = 16
SEQ = 2048
DEPTH = 2

BRANCH_WIDTH = 512
N_BRANCHES = 3
NORM_EPS = 1e-6
RWKV_HEADS = 8
RWKV_HEAD_DIM = 64
RWKV_WIDTH = RWKV_HEADS * RWKV_HEAD_DIM
RWKV_DECAY_LORA = 64
RWKV_AAA_LORA = 64
RWKV_VRES_LORA = 32
RWKV_GN_EPS = 64e-5
RWKV_SHIFT_SIZES = (RWKV_WIDTH, RWKV_WIDTH, RWKV_WIDTH, RWKV_DECAY_LORA, RWKV_AAA_LORA)
RWKV_SHIFT_COLS = 3 * RWKV_WIDTH + RWKV_DECAY_LORA + RWKV_AAA_LORA
GMLP_WIDTH = 512
GMLP_GROUPS = 8
GMLP_GROUP_DIM = GMLP_WIDTH // GMLP_GROUPS
GMLP_CHUNK = 128
GMLP_LN_EPS = 1e-5
MLA_HEADS = 8
MLA_Q_LORA = 384
MLA_KV_LORA = 256
MLA_NOPE = 64
MLA_ROPE = 32
MLA_V = 64
MLA_WIDTH = MLA_HEADS * MLA_V
ROPE_THETA = 10000.0
ATTN_BLOCK = 128
IN_SEG_SIZES = (RWKV_SHIFT_COLS, RWKV_WIDTH,
                GMLP_WIDTH, GMLP_WIDTH, GMLP_WIDTH,
                MLA_Q_LORA, MLA_KV_LORA, MLA_ROPE, MLA_WIDTH,
                N_BRANCHES * D_MODEL)
IN_COLS = sum(IN_SEG_SIZES)

kernel_name = 'hybrid_rwkv7_gmlp_mla_gated_block'


def _offsets(sizes):
    return [int(o) for o in np.cumsum(sizes)[:-1]]


def _rmsnorm(x, gain):
    x32 = x.astype(jnp.float32)
    y = x32 * lax.rsqrt(jnp.mean(x32 * x32, axis=-1, keepdims=True) + NORM_EPS)
    return (y * gain.astype(jnp.float32)).astype(x.dtype)


def _layernorm(x, w, b, eps):
    x32 = x.astype(jnp.float32)
    mu = jnp.mean(x32, axis=-1, keepdims=True)
    xc = x32 - mu
    var = jnp.mean(xc * xc, axis=-1, keepdims=True)
    return (xc * lax.rsqrt(var + eps) * w.astype(jnp.float32) + b.astype(jnp.float32)).astype(x.dtype)


def _wkv7_step(state, inp):
    r, w, k, v, a, b = inp
    sa = jnp.einsum('bhij,bhj->bhi', state, a)
    state = state * w[:, :, None, :] + sa[..., None] * b[:, :, None, :] + v[..., None] * k[:, :, None, :]
    y = jnp.einsum('bhij,bhj->bhi', state, r)
    return state, y


def _rwkv7_mix(p, v_first, vres_gate, mu, w0, w_up, a0, a_up, k_k, k_a, r_k, ln_w, ln_b):
    B, T, _ = p.shape
    dt = p.dtype
    f32 = jnp.float32
    prev = jnp.pad(p, ((0, 0), (1, 0), (0, 0)))[:, :-1]
    p = p + (prev - p) * mu
    r, k, v, w_dn, a_dn = jnp.split(p, _offsets(RWKV_SHIFT_SIZES), axis=-1)
    w_log = -jax.nn.softplus(-(w0 + jnp.tanh(w_dn) @ w_up).astype(f32)) - 0.5
    decay = jnp.exp(-jnp.exp(w_log))
    a = jax.nn.sigmoid((a0 + a_dn @ a_up).astype(f32))
    if vres_gate is not None:
        v = v + (v_first - v) * vres_gate
    v_out = v.astype(dt)
    hs = (B, T, RWKV_HEADS, RWKV_HEAD_DIM)
    kk = (k * k_k).astype(f32).reshape(hs)
    kk = kk / jnp.maximum(jnp.sqrt(jnp.sum(kk * kk, axis=-1, keepdims=True)), 1e-12)
    k = (k.astype(f32) * (1.0 + (a - 1.0) * k_a.astype(f32))).reshape(hs)
    r = r.astype(f32).reshape(hs)
    v = v.astype(f32).reshape(hs)
    decay = decay.reshape(hs)
    a = a.reshape(hs)
    tm = lambda t: jnp.swapaxes(t, 0, 1)
    state0 = jnp.zeros((B, RWKV_HEADS, RWKV_HEAD_DIM, RWKV_HEAD_DIM), f32)
    _, y = lax.scan(_wkv7_step, state0, (tm(r), tm(decay), tm(k), tm(v), tm(-kk), tm(kk * a)))
    y = tm(y)
    y = _layernorm(y, ln_w.reshape(RWKV_HEADS, RWKV_HEAD_DIM), ln_b.reshape(RWKV_HEADS, RWKV_HEAD_DIM), RWKV_GN_EPS)
    y = y + jnp.sum(r * k * r_k.astype(f32), axis=-1, keepdims=True) * v
    return y.reshape(B, T, RWKV_WIDTH).astype(dt), v_out


def _gmlp_mix(u, v, ln_w, ln_b, w_s, b_s):
    B, T, _ = u.shape
    u = jax.nn.gelu(u, approximate=False)
    v = _layernorm(jax.nn.gelu(v, approximate=False), ln_w, ln_b, GMLP_LN_EPS)
    vc = v.reshape(B, T // GMLP_CHUNK, GMLP_CHUNK, GMLP_GROUPS, GMLP_GROUP_DIM)
    causal = jnp.tril(jnp.ones((GMLP_CHUNK, GMLP_CHUNK), dtype=bool))
    w = jnp.where(causal[None], w_s, jnp.zeros((), w_s.dtype))
    vm = jnp.einsum('gts,bnsgc->bntgc', w, vc) + b_s.T[:, :, None]
    return u * vm.reshape(B, T, GMLP_WIDTH)


def _apply_rope(x, cos, sin):
    half = x.shape[-1] // 2
    x1 = x[..., :half].astype(jnp.float32)
    x2 = x[..., half:].astype(jnp.float32)
    return jnp.concatenate([x1 * cos - x2 * sin, x2 * cos + x1 * sin], axis=-1).astype(x.dtype)


def _mla_mix(c_q, c_kv, k_rope, positions, q_norm, w_uq, kv_norm, w_ukv):
    B, T, _ = c_q.shape
    q = (_rmsnorm(c_q, q_norm) @ w_uq).reshape(B, T, MLA_HEADS, MLA_NOPE + MLA_ROPE)
    kv = (_rmsnorm(c_kv, kv_norm) @ w_ukv).reshape(B, T, MLA_HEADS, MLA_NOPE + MLA_V)
    q_nope, q_rope = q[..., :MLA_NOPE], q[..., MLA_NOPE:]
    k_nope, v = kv[..., :MLA_NOPE], kv[..., MLA_NOPE:]
    inv_freq = 1.0 / (ROPE_THETA ** (jnp.arange(0, MLA_ROPE, 2, dtype=jnp.float32) / MLA_ROPE))
    ang = positions.astype(jnp.float32)[..., None] * inv_freq
    cos, sin = jnp.cos(ang), jnp.sin(ang)
    q_rope = _apply_rope(q_rope, cos[:, :, None, :], sin[:, :, None, :])
    k_rope = _apply_rope(k_rope, cos, sin)
    q = jnp.concatenate([q_nope, q_rope], axis=-1)
    k = jnp.concatenate([k_nope, jnp.broadcast_to(k_rope[:, :, None, :], (B, T, MLA_HEADS, MLA_ROPE))], axis=-1)
    scale = (MLA_NOPE + MLA_ROPE) ** -0.5
    outs = []
    for i in range(T // ATTN_BLOCK):
        q0, q1 = i * ATTN_BLOCK, (i + 1) * ATTN_BLOCK
        s = jnp.einsum('bqhd,bkhd->bhqk', q[:, q0:q1], k[:, :q1]).astype(jnp.float32) * scale
        causal = jnp.arange(q1)[None, :] <= jnp.arange(q0, q1)[:, None]
        s = jnp.where(causal, s, -jnp.inf)
        pr = jax.nn.softmax(s, axis=-1).astype(v.dtype)
        outs.append(jnp.einsum('bhqk,bkhd->bqhd', pr, v[:, :q1]))
    return jnp.concatenate(outs, axis=1).reshape(B, T, MLA_WIDTH)


def setup_inputs(seed: int = 0) -> dict:
    key = jax.random.key(seed)
    ks = jax.random.split(key, 28)
    nrm = lambda k, shape, s: jax.random.normal(k, shape, jnp.float32) * s
    L, Lv = DEPTH, max(DEPTH - 1, 0)
    offs = jax.random.randint(ks[1], (BATCH, 1), 0, 4096)
    return {
        'x': nrm(ks[0], (BATCH, SEQ, D_MODEL), 1.0),
        'positions': (offs + jnp.arange(SEQ)[None, :]).astype(jnp.int32),
        'w_in': nrm(ks[2], (L, D_MODEL, IN_COLS), D_MODEL ** -0.5),
        'pre_norm': 1.0 + nrm(ks[3], (L, D_MODEL), 0.1),
        'post_norm': 1.0 + nrm(ks[4], (L, D_MODEL), 0.1),
        'rwkv_mu': jax.random.uniform(ks[5], (L, RWKV_SHIFT_COLS), jnp.float32),
        'rwkv_w0': -2.0 + nrm(ks[6], (L, RWKV_WIDTH), 0.5),
        'rwkv_w_up': nrm(ks[7], (L, RWKV_DECAY_LORA, RWKV_WIDTH), RWKV_DECAY_LORA ** -0.5),
        'rwkv_a0': nrm(ks[8], (L, RWKV_WIDTH), 0.5),
        'rwkv_a_up': nrm(ks[9], (L, RWKV_AAA_LORA, RWKV_WIDTH), RWKV_AAA_LORA ** -0.5),
        'rwkv_k_k': 0.85 + nrm(ks[10], (L, RWKV_WIDTH), 0.1),
        'rwkv_k_a': 1.0 + nrm(ks[11], (L, RWKV_WIDTH), 0.1),
        'rwkv_r_k': nrm(ks[12], (L, RWKV_HEADS, RWKV_HEAD_DIM), 0.1),
        'rwkv_ln_w': 1.0 + nrm(ks[13], (L, RWKV_WIDTH), 0.1),
        'rwkv_ln_b': nrm(ks[14], (L, RWKV_WIDTH), 0.02),
        'rwkv_v0': nrm(ks[15], (Lv, RWKV_WIDTH), 0.5),
        'rwkv_v_down': nrm(ks[16], (Lv, D_MODEL, RWKV_VRES_LORA), D_MODEL ** -0.5),
        'rwkv_v_up': nrm(ks[17], (Lv, RWKV_VRES_LORA, RWKV_WIDTH), RWKV_VRES_LORA ** -0.5),
        'gmlp_ln_w': 1.0 + nrm(ks[18], (L, GMLP_WIDTH), 0.1),
        'gmlp_ln_b': nrm(ks[19], (L, GMLP_WIDTH), 0.02),
        'gmlp_w_s': nrm(ks[20], (L, GMLP_GROUPS, GMLP_CHUNK, GMLP_CHUNK), GMLP_CHUNK ** -0.5),
        'gmlp_b_s': 1.0 + nrm(ks[21], (L, GMLP_GROUPS, GMLP_CHUNK), 0.1),
        'mla_q_norm': 1.0 + nrm(ks[22], (L, MLA_Q_LORA), 0.1),
        'mla_w_uq': nrm(ks[23], (L, MLA_Q_LORA, MLA_HEADS * (MLA_NOPE + MLA_ROPE)), MLA_Q_LORA ** -0.5),
        'mla_kv_norm': 1.0 + nrm(ks[24], (L, MLA_KV_LORA), 0.1),
        'mla_w_ukv': nrm(ks[25], (L, MLA_KV_LORA, MLA_HEADS * (MLA_NOPE + MLA_V)), MLA_KV_LORA ** -0.5),
        'w_branch': nrm(ks[26], (L, N_BRANCHES, BRANCH_WIDTH, D_MODEL), BRANCH_WIDTH ** -0.5),
        'w_out': nrm(ks[27], (L, D_MODEL, D_MODEL), D_MODEL ** -0.5),
    }


def reference(x, positions, w_in, pre_norm, post_norm, rwkv_mu, rwkv_w0, rwkv_w_up, rwkv_a0, rwkv_a_up,
              rwkv_k_k, rwkv_k_a, rwkv_r_k, rwkv_ln_w, rwkv_ln_b, rwkv_v0, rwkv_v_down, rwkv_v_up,
              gmlp_ln_w, gmlp_ln_b, gmlp_w_s, gmlp_b_s, mla_q_norm, mla_w_uq, mla_kv_norm, mla_w_ukv,
              w_branch, w_out):
    B, T, D = x.shape
    v_first = None
    for l in range(DEPTH):
        h = _rmsnorm(x, pre_norm[l])
        p = h @ w_in[l]
        (rw_in, rw_gate, g_u, g_v, g_gate, c_q, c_kv, k_rope, mla_gate, merge) = jnp.split(
            p, _offsets(IN_SEG_SIZES), axis=-1)
        vres_gate = None
        if l > 0:
            vres_gate = jax.nn.sigmoid((rwkv_v0[l - 1] + (h @ rwkv_v_down[l - 1]) @ rwkv_v_up[l - 1]).astype(jnp.float32))
        y_a, v_l = _rwkv7_mix(rw_in, v_first, vres_gate, rwkv_mu[l], rwkv_w0[l], rwkv_w_up[l], rwkv_a0[l],
                              rwkv_a_up[l], rwkv_k_k[l], rwkv_k_a[l], rwkv_r_k[l], rwkv_ln_w[l], rwkv_ln_b[l])
        if l == 0:
            v_first = v_l
        y_b = _gmlp_mix(g_u, g_v, gmlp_ln_w[l], gmlp_ln_b[l], gmlp_w_s[l], gmlp_b_s[l])
        y_c = _mla_mix(c_q, c_kv, k_rope, positions, mla_q_norm[l], mla_w_uq[l], mla_kv_norm[l], mla_w_ukv[l])
        branches = jnp.stack([y_a * jax.nn.silu(rw_gate),
                              y_b * jax.nn.silu(g_gate),
                              y_c * jax.nn.silu(mla_gate)], axis=2)
        proj = jnp.einsum('btnk,nkd->btnd', branches, w_branch[l])
        gates = jax.nn.sigmoid(merge.reshape(B, T, N_BRANCHES, D))
        mixed = jnp.sum(gates * proj, axis=2) @ w_out[l]
        x = x + _rmsnorm(mixed, post_norm[l])
    return x
```

```python
import functools

import numpy as np
import jax
import jax.numpy as jnp
from jax import lax
from jax.experimental import pallas as pl
from jax.experimental.pallas import tpu as pltpu

F32 = jnp.float32
BF16 = jnp.bfloat16

D_MODEL = 1024
BRANCH_WIDTH = 512
N_BRANCHES = 3
NORM_EPS = 1e-6
RWKV_HEADS = 8
RWKV_HEAD_DIM = 64
RWKV_WIDTH = RWKV_HEADS * RWKV_HEAD_DIM
RWKV_DECAY_LORA = 64
RWKV_AAA_LORA = 64
RWKV_VRES_LORA = 32
RWKV_GN_EPS = 64e-5
RWKV_SHIFT_COLS = 3 * RWKV_WIDTH + RWKV_DECAY_LORA + RWKV_AAA_LORA
GMLP_WIDTH = 512
GMLP_GROUPS = 8
GMLP_CHUNK = 128
GMLP_LN_EPS = 1e-5
MLA_HEADS = 8
MLA_Q_LORA = 384
MLA_KV_LORA = 256
MLA_NOPE = 64
MLA_ROPE = 32
MLA_V = 64
MLA_WIDTH = MLA_HEADS * MLA_V
ROPE_THETA = 10000.0

LANES = 128
MISC = LANES
RW_COLS = RWKV_SHIFT_COLS + RWKV_WIDTH + MISC
G_COLS = 3 * GMLP_WIDTH
MLA_COLS = MLA_WIDTH + MLA_Q_LORA + MLA_KV_LORA + MISC
MG_COLS = N_BRANCHES * D_MODEL
ALL_COLS = RW_COLS + G_COLS + MLA_COLS + MG_COLS
HEAD_PAD = LANES
ROPE_LANE0 = MLA_NOPE
SUB = 64
PAIR = 2 * SUB

VMEM_LIMIT = 56 * 1024 * 1024


def _dot(a, b):
    return jnp.dot(a, b, preferred_element_type=F32)


def _dot_nt(a, b):
    return lax.dot_general(a, b, (((1,), (1,)), ((), ())), preferred_element_type=F32)


def _dot_tn(a, b):
    return lax.dot_general(a, b, (((0,), (0,)), ((), ())), preferred_element_type=F32)


def _split3(x):
    hi = x.astype(BF16)
    r1 = x - hi.astype(F32)
    mid = r1.astype(BF16)
    lo = (r1 - mid.astype(F32)).astype(BF16)
    return hi, mid, lo


def _dot_exact_lhs(m, x):
    hi, mid, lo = _split3(x)
    return _dot(m, hi) + _dot(m, mid) + _dot(m, lo)


def _dot_exact_rhs(x, m):
    hi, mid, lo = _split3(x)
    return _dot(hi, m) + _dot(mid, m) + _dot(lo, m)


def _const_spec(shape):
    nd = len(shape)
    return pl.BlockSpec(shape, lambda *_: (0,) * nd)


def _in_proj_kernel(x_ref, g_ref, w_ref, o_rw, o_g, o_mla, o_mg):
    x = x_ref[...]
    ms = jnp.mean(x * x, axis=-1, keepdims=True)
    h = (x * lax.rsqrt(ms + NORM_EPS) * g_ref[...]).astype(BF16)
    col = 0
    for o in (o_rw, o_g, o_mla, o_mg):
        width = o.shape[1]
        for c0 in range(0, width, 512):
            cw = min(512, width - c0)
            o[:, c0:c0 + cw] = _dot(h, w_ref[:, col + c0:col + c0 + cw]).astype(o.dtype)
        col += width


def _in_proj(x2, gain, w_all, tm):
    n = x2.shape[0]
    outs = (RW_COLS, G_COLS, MLA_COLS, MG_COLS)
    return pl.pallas_call(
        _in_proj_kernel,
        grid=(n // tm,),
        in_specs=[pl.BlockSpec((tm, D_MODEL), lambda i: (i, 0)),
                  _const_spec((1, D_MODEL)),
                  pl.BlockSpec((D_MODEL, ALL_COLS), lambda i: (0, 0), pipeline_mode=pl.Buffered(1))],
        out_specs=[pl.BlockSpec((tm, c), lambda i: (i, 0)) for c in outs],
        out_shape=[jax.ShapeDtypeStruct((n, c), BF16) for c in outs],
        compiler_params=pltpu.CompilerParams(dimension_semantics=("arbitrary",), vmem_limit_bytes=VMEM_LIMIT),
        name="in_proj",
    )(x2, gain, w_all)


def _rwkv_kernel(has_vres, tc, *refs):
    it = iter(refs)
    rw_ref = next(it)
    vf_ref = next(it) if has_vres else None
    mu_ref, w0_ref, wup_ref, a0_ref, aup_ref, kk_ref, ka_ref, rk_ref, lnw_ref, lnb_ref = (next(it) for _ in range(10))
    if has_vres:
        v0_ref, vup_ref = next(it), next(it)
    hsum_ref, cum_ref = next(it), next(it)
    y_ref = next(it)
    vout_ref = None if has_vres else next(it)
    carry_ref, state_ref = next(it), next(it)
    at_s, rt_s, bt_s, kt_s, v_s, bd_s, kd_s = (next(it) for _ in range(7))
    pc_s, y_s = next(it), next(it)

    @pl.when(pl.program_id(1) == 0)
    def _():
        carry_ref[...] = jnp.zeros_like(carry_ref)
        state_ref[...] = jnp.zeros_like(state_ref)

    p = rw_ref[:, 0:RWKV_SHIFT_COLS].astype(F32)
    row = lax.broadcasted_iota(jnp.int32, p.shape, 0)
    prev = jnp.where(row == 0, carry_ref[0:1, :], pltpu.roll(p, 1, axis=0))
    carry_ref[0:1, :] = p[tc - 1:tc, :]
    p = p + (prev - p) * mu_ref[...]
    r = p[:, 0:RWKV_WIDTH]
    k = p[:, RWKV_WIDTH:2 * RWKV_WIDTH]
    v = p[:, 2 * RWKV_WIDTH:3 * RWKV_WIDTH]
    lora_in = p[:, 3 * RWKV_WIDTH:RWKV_SHIFT_COLS]

    z = -(w0_ref[...] + _dot(jnp.tanh(lora_in).astype(BF16), wup_ref[...]))
    softplus = jnp.maximum(z, 0.0) + jnp.log1p(jnp.exp(-jnp.abs(z)))
    lw = -jnp.exp(-softplus - 0.5)
    a = jax.nn.sigmoid(a0_ref[...] + _dot(lora_in.astype(BF16), aup_ref[...]))
    if has_vres:
        vd = rw_ref[:, RWKV_SHIFT_COLS + RWKV_WIDTH:RW_COLS]
        vg = jax.nn.sigmoid(v0_ref[...] + _dot(vd, vup_ref[...]))
        v = v + (vf_ref[...] - v) * vg
    else:
        vout_ref[...] = v

    hsum = hsum_ref[...]
    kk = k * kk_ref[...]
    kk = kk * lax.rsqrt(jnp.maximum(_dot_exact_rhs(kk * kk, hsum), 1e-24))
    k2 = k * (1.0 + (a - 1.0) * ka_ref[...])
    av = -kk
    bv = kk * a

    gs = _dot_exact_lhs(cum_ref[...], lw)
    g = gs[0:tc]
    gl = gs[tc:2 * tc]
    e_neg = jnp.exp(-g)
    e_end = jnp.exp(gl - g)
    lane = lax.broadcasted_iota(jnp.int32, (tc, RWKV_WIDTH), 1)
    first = (lane % LANES) < RWKV_HEAD_DIM

    def put(ref, x):
        ref[0] = jnp.where(first, x, 0.0).astype(BF16)
        ref[1] = jnp.where(first, 0.0, x).astype(BF16)

    put(at_s, av * jnp.exp(g - lw))
    put(rt_s, r * jnp.exp(g))
    put(bt_s, bv * e_neg)
    put(kt_s, k2 * e_neg)
    put(v_s, v)
    put(bd_s, bv * e_end)
    put(kd_s, k2 * e_end)
    pc_s[...] = jnp.exp(gl)

    ri = lax.broadcasted_iota(jnp.int32, (PAIR, PAIR), 0)
    ci = lax.broadcasted_iota(jnp.int32, (PAIR, PAIR), 1)
    strict = ri > ci
    incl = ri >= ci
    eye = jnp.where(ri == ci, 1.0, 0.0).astype(F32)
    levels = (4, 8, 16, 32, 64)
    lvl_mask = [strict & ((ri // b) == (ci // b)) & ((ri // (b // 2)) != (ci // (b // 2))) for b in levels]
    lvl2 = strict & ((ri // 2) == (ci // 2))

    def stacked(ref, r0, ls):
        return jnp.concatenate([ref[0, pl.ds(r0, SUB), ls], ref[1, pl.ds(r0, SUB), ls]], axis=0)

    def sub_chunk(c, carry):
        r0 = pl.multiple_of(c * SUB, SUB)
        for j in range(RWKV_HEADS // 2):
            ls = slice(LANES * j, LANES * (j + 1))
            at, rt, bt, kt, vv, bd, kd = (stacked(s, r0, ls) for s in (at_s, rt_s, bt_s, kt_s, v_s, bd_s, kd_s))
            s = _dot_nt(jnp.concatenate([at, rt], axis=0), jnp.concatenate([bt, kt], axis=0))
            nab = jnp.where(strict, s[0:PAIR, 0:PAIR], 0.0)
            aak = jnp.where(strict, s[0:PAIR, PAIR:2 * PAIR], 0.0).astype(BF16)
            rb = jnp.where(incl, s[PAIR:2 * PAIR, 0:PAIR], 0.0).astype(BF16)
            rk = jnp.where(incl, s[PAIR:2 * PAIR, PAIR:2 * PAIR], 0.0).astype(BF16)
            d = eye + jnp.where(lvl2, nab, 0.0)
            for m in lvl_mask:
                e = jnp.where(m, nab, 0.0).astype(BF16)
                db = d.astype(BF16)
                d = d + _dot(_dot(db, e).astype(BF16), db)
            t = d.astype(BF16)
            w1 = _dot(aak, vv).astype(BF16)
            au = _dot(t, jnp.concatenate([at, w1], axis=1))
            ah = au[:, 0:PAIR].astype(BF16)
            uh = au[:, PAIR:2 * PAIR]
            kv = _dot_tn(kd, vv)
            h0 = state_ref[j]
            arh = _dot(jnp.concatenate([ah, rt], axis=0), h0.astype(BF16))
            u = (arh[0:PAIR] + uh).astype(BF16)
            pc = pc_s[pl.ds(r0, SUB), ls]
            pcm = jnp.concatenate([pc, pc], axis=0).T
            state_ref[j] = pcm * h0 + _dot_tn(bd, u) + kv
            y = arh[PAIR:2 * PAIR] + _dot(jnp.concatenate([rb, rk], axis=1), jnp.concatenate([u, vv], axis=0))
            y_s[pl.ds(r0, SUB), ls] = y[0:SUB] + y[SUB:PAIR]
        return carry

    lax.fori_loop(0, tc // SUB, sub_chunk, 0)

    y = y_s[...]
    inv_n = 1.0 / RWKV_HEAD_DIM
    mean = _dot_exact_rhs(y, hsum) * inv_n
    yc = y - mean
    var = _dot_exact_rhs(yc * yc, hsum) * inv_n
    y = yc * lax.rsqrt(var + RWKV_GN_EPS) * lnw_ref[...] + lnb_ref[...]
    y = y + _dot_exact_rhs(r * k2 * rk_ref[...], hsum) * v
    gate = rw_ref[:, RWKV_SHIFT_COLS:RWKV_SHIFT_COLS + RWKV_WIDTH].astype(F32)
    y_ref[...] = (y * gate * jax.nn.sigmoid(gate)).astype(y_ref.dtype)


def _rwkv(o_rw, v_first, prm, consts, bsz, seq, tc):
    has_vres = v_first is not None
    n = bsz * seq
    nt = seq // tc
    row = lambda b, t: (b * nt + t, 0)
    args = [o_rw]
    specs = [pl.BlockSpec((tc, RW_COLS), row)]
    if has_vres:
        args.append(v_first)
        specs.append(pl.BlockSpec((tc, RWKV_WIDTH), row))
    names = ["mu", "w0", "w_up", "a0", "a_up", "k_k", "k_a", "r_k", "ln_w", "ln_b"]
    if has_vres:
        names += ["v0", "v_up"]
    for nm in names:
        args.append(prm[nm])
        specs.append(_const_spec(prm[nm].shape))
    for c in (consts["hsum"], consts["cum"]):
        args.append(c)
        specs.append(_const_spec(c.shape))
    out_shape = [jax.ShapeDtypeStruct((n, RWKV_WIDTH), BF16)]
    out_specs = [pl.BlockSpec((tc, RWKV_WIDTH), row)]
    if not has_vres:
        out_shape.append(jax.ShapeDtypeStruct((n, RWKV_WIDTH), F32))
        out_specs.append(pl.BlockSpec((tc, RWKV_WIDTH), row))
    scratch = [pltpu.VMEM((8, RWKV_SHIFT_COLS), F32),
               pltpu.VMEM((RWKV_HEADS // 2, PAIR, PAIR), F32)]
    scratch += [pltpu.VMEM((2, tc, RWKV_WIDTH), BF16) for _ in range(7)]
    scratch += [pltpu.VMEM((tc, RWKV_WIDTH), F32), pltpu.VMEM((tc, RWKV_WIDTH), F32)]
    res = pl.pallas_call(
        functools.partial(_rwkv_kernel, has_vres, tc),
        grid=(bsz, nt),
        in_specs=specs,
        out_specs=out_specs,
        out_shape=out_shape,
        scratch_shapes=scratch,
        compiler_params=pltpu.CompilerParams(dimension_semantics=("arbitrary", "arbitrary"),
                                             vmem_limit_bytes=VMEM_LIMIT),
        name="rwkv_vres" if has_vres else "rwkv",
    )(*args)
    return (res[0], v_first) if has_vres else (res[0], res[1])


def _gelu(x):
    return 0.5 * x * (1.0 + lax.erf(x * np.float32(1.0 / np.sqrt(2.0))))


def _gmlp_kernel(tm, g_ref, lnw_ref, lnb_ref, ws_ref, bias_ref, o_ref):
    u = _gelu(g_ref[:, 0:GMLP_WIDTH].astype(F32))
    v = _gelu(g_ref[:, GMLP_WIDTH:2 * GMLP_WIDTH].astype(F32))
    mu = jnp.mean(v, axis=-1, keepdims=True)
    vc = v - mu
    var = jnp.mean(vc * vc, axis=-1, keepdims=True)
    v = vc * lax.rsqrt(var + GMLP_LN_EPS) * lnw_ref[...] + lnb_ref[...]
    gate = g_ref[:, 2 * GMLP_WIDTH:3 * GMLP_WIDTH].astype(F32)
    ug = u * gate * jax.nn.sigmoid(gate)
    lane = lax.broadcasted_iota(jnp.int32, (GMLP_CHUNK, LANES), 1)
    first = lane < (LANES // 2)
    ri = lax.broadcasted_iota(jnp.int32, (GMLP_CHUNK, 2 * GMLP_CHUNK), 0)
    ci = lax.broadcasted_iota(jnp.int32, (GMLP_CHUNK, 2 * GMLP_CHUNK), 1)
    causal = ri >= (ci % GMLP_CHUNK)
    for j in range(GMLP_GROUPS // 2):
        w = jnp.where(causal, ws_ref[j], 0.0).astype(BF16)
        ls = slice(LANES * j, LANES * (j + 1))
        for c in range(tm // GMLP_CHUNK):
            rs = slice(GMLP_CHUNK * c, GMLP_CHUNK * (c + 1))
            vp = v[rs, ls]
            stacked = jnp.concatenate([jnp.where(first, vp, 0.0), jnp.where(first, 0.0, vp)], axis=0).astype(BF16)
            vm = _dot(w, stacked) + bias_ref[:, ls]
            o_ref[rs, ls] = (ug[rs, ls] * vm).astype(o_ref.dtype)


def _gmlp(o_g, prm, tm):
    n = o_g.shape[0]
    return pl.pallas_call(
        functools.partial(_gmlp_kernel, tm),
        grid=(n // tm,),
        in_specs=[pl.BlockSpec((tm, G_COLS), lambda i: (i, 0)),
                  _const_spec((1, GMLP_WIDTH)), _const_spec((1, GMLP_WIDTH)),
                  _const_spec(prm["w_s"].shape), _const_spec(prm["bias"].shape)],
        out_specs=pl.BlockSpec((tm, GMLP_WIDTH), lambda i: (i, 0)),
        out_shape=jax.ShapeDtypeStruct((n, GMLP_WIDTH), BF16),
        compiler_params=pltpu.CompilerParams(dimension_semantics=("arbitrary",), vmem_limit_bytes=VMEM_LIMIT),
        name="gmlp",
    )(o_g, prm["ln_w"], prm["ln_b"], prm["w_s"], prm["bias"])


def _rms(x, gain):
    return x * lax.rsqrt(jnp.mean(x * x, axis=-1, keepdims=True) + NORM_EPS) * gain


def _mla_prep_kernel(m_ref, pos_ref, invf_ref, qn_ref, wq_ref, kvn_ref, wkv_ref, q_ref, k_ref, v_ref):
    c0 = MLA_WIDTH
    c_q = m_ref[:, c0:c0 + MLA_Q_LORA].astype(F32)
    c_kv = m_ref[:, c0 + MLA_Q_LORA:c0 + MLA_Q_LORA + MLA_KV_LORA].astype(F32)
    k_rope = m_ref[:, c0 + MLA_Q_LORA + MLA_KV_LORA:MLA_COLS].astype(F32)
    ang = pos_ref[...].astype(F32) * invf_ref[...]
    cos = jnp.cos(ang)
    sin = jnp.sin(ang)
    lane = lax.broadcasted_iota(jnp.int32, ang.shape, 1)
    half = MLA_ROPE // 2
    sin_hi = jnp.where((lane >= ROPE_LANE0 + half) & (lane < ROPE_LANE0 + MLA_ROPE), sin, 0.0)
    sin_lo = jnp.where((lane >= ROPE_LANE0) & (lane < ROPE_LANE0 + half), -sin, 0.0)

    def rope(x):
        return x * cos + pltpu.roll(x, half, axis=1) * sin_hi + pltpu.roll(x, LANES - half, axis=1) * sin_lo

    scale = np.float32((MLA_NOPE + MLA_ROPE) ** -0.5)
    q = _dot(_rms(c_q, qn_ref[...]).astype(BF16), wq_ref[...])
    kv = _dot(_rms(c_kv, kvn_ref[...]).astype(BF16), wkv_ref[...])
    kr = rope(k_rope)
    for h in range(MLA_HEADS):
        ls = slice(HEAD_PAD * h, HEAD_PAD * (h + 1))
        q_ref[:, ls] = (rope(q[:, ls]) * scale).astype(BF16)
        k_ref[:, ls] = (kv[:, ls] + kr).astype(BF16)
    v_ref[...] = kv[:, MLA_HEADS * HEAD_PAD:].astype(BF16)


def _mla_prep(o_mla, pos128, prm, tm):
    n = o_mla.shape[0]
    wide = MLA_HEADS * HEAD_PAD
    return pl.pallas_call(
        _mla_prep_kernel,
        grid=(n // tm,),
        in_specs=[pl.BlockSpec((tm, MLA_COLS), lambda i: (i, 0)),
                  pl.BlockSpec((tm, LANES), lambda i: (i, 0)),
                  _const_spec((1, LANES)),
                  _const_spec((1, MLA_Q_LORA)), _const_spec(prm["w_uq"].shape),
                  _const_spec((1, MLA_KV_LORA)), _const_spec(prm["w_ukv"].shape)],
        out_specs=[pl.BlockSpec((tm, wide), lambda i: (i, 0))] * 3,
        out_shape=[jax.ShapeDtypeStruct((n, wide), BF16)] * 3,
        compiler_params=pltpu.CompilerParams(dimension_semantics=("arbitrary",), vmem_limit_bytes=VMEM_LIMIT),
        name="mla_prep",
    )(o_mla, pos128, prm["inv_freq"], prm["q_norm"], prm["w_uq"], prm["kv_norm"], prm["w_ukv"])


def _attn_kernel(tq, q_ref, k_ref, v_ref, gate_ref, o_ref):
    i = pl.program_id(2)
    ri = lax.broadcasted_iota(jnp.int32, (tq, tq), 0)
    ci = lax.broadcasted_iota(jnp.int32, (tq, tq), 1)
    causal = ci <= ri
    out = jnp.zeros((tq, LANES), F32)
    for hh in range(2):
        ls = slice(HEAD_PAD * hh, HEAD_PAD * (hh + 1))
        q = q_ref[:, ls]

        def block(j, carry, masked):
            m, l, acc = carry
            r0 = pl.multiple_of(j * tq, tq)
            s = _dot_nt(q, k_ref[pl.ds(r0, tq), ls])
            if masked:
                s = jnp.where(causal, s, -jnp.inf)
            m_new = jnp.maximum(m, jnp.max(s, axis=-1, keepdims=True))
            alpha = jnp.exp(m - m_new)
            p = jnp.exp(s - m_new)
            l = alpha * l + jnp.sum(p, axis=-1, keepdims=True)
            acc = alpha * acc + _dot(p.astype(BF16), v_ref[pl.ds(r0, tq), ls])
            return m_new, l, acc

        init = (jnp.full((tq, 1), -jnp.inf, F32), jnp.zeros((tq, 1), F32), jnp.zeros((tq, LANES), F32))
        carry = lax.fori_loop(0, i, functools.partial(block, masked=False), init)
        m, l, acc = block(i, carry, True)
        out = out + acc * (1.0 / l)
    gate = gate_ref[...].astype(F32)
    o_ref[...] = (out * gate * jax.nn.sigmoid(gate)).astype(o_ref.dtype)


def _attn(q, k, vz, o_mla, bsz, seq, tq):
    n = bsz * seq
    nq = seq // tq
    pairs = MLA_HEADS // 2
    return pl.pallas_call(
        functools.partial(_attn_kernel, tq),
        grid=(bsz, pairs, nq),
        in_specs=[pl.BlockSpec((tq, 2 * HEAD_PAD), lambda b, h, i: (b * nq + i, h)),
                  pl.BlockSpec((seq, 2 * HEAD_PAD), lambda b, h, i: (b, h)),
                  pl.BlockSpec((seq, 2 * HEAD_PAD), lambda b, h, i: (b, h)),
                  pl.BlockSpec((tq, LANES), lambda b, h, i: (b * nq + i, h))],
        out_specs=pl.BlockSpec((tq, LANES), lambda b, h, i: (b * nq + i, h)),
        out_shape=jax.ShapeDtypeStruct((n, MLA_WIDTH), BF16),
        compiler_params=pltpu.CompilerParams(dimension_semantics=("arbitrary",) * 3, vmem_limit_bytes=VMEM_LIMIT),
        name="attn",
    )(q, k, vz, o_mla)


def _merge_kernel(ya_ref, yb_ref, yc_ref, mg_ref, x_ref, wb_ref, wo_ref, pn_ref, o_ref):
    ys = (ya_ref[...], yb_ref[...], yc_ref[...])
    halves = []
    for c in range(D_MODEL // 512):
        acc = None
        for nb in range(N_BRANCHES):
            cols = slice(nb * D_MODEL + c * 512, nb * D_MODEL + (c + 1) * 512)
            term = jax.nn.sigmoid(mg_ref[:, cols].astype(F32)) * _dot(ys[nb], wb_ref[nb, :, c * 512:(c + 1) * 512])
            acc = term if acc is None else acc + term
        halves.append(acc.astype(BF16))
    mixed = _dot(jnp.concatenate(halves, axis=1), wo_ref[...])
    ms = jnp.mean(mixed * mixed, axis=-1, keepdims=True)
    o_ref[...] = x_ref[...] + mixed * lax.rsqrt(ms + NORM_EPS) * pn_ref[...]


def _merge(ya, yb, yc, o_mg, x2, w_branch, w_out, post_norm, tm):
    n = x2.shape[0]
    rowspec = lambda c: pl.BlockSpec((tm, c), lambda i: (i, 0))
    return pl.pallas_call(
        _merge_kernel,
        grid=(n // tm,),
        in_specs=[rowspec(BRANCH_WIDTH), rowspec(BRANCH_WIDTH), rowspec(BRANCH_WIDTH), rowspec(MG_COLS),
                  rowspec(D_MODEL), _const_spec(w_branch.shape), _const_spec(w_out.shape), _const_spec((1, D_MODEL))],
        out_specs=rowspec(D_MODEL),
        out_shape=jax.ShapeDtypeStruct((n, D_MODEL), F32),
        compiler_params=pltpu.CompilerParams(dimension_semantics=("arbitrary",), vmem_limit_bytes=VMEM_LIMIT),
        name="merge",
    )(ya, yb, yc, o_mg, x2, w_branch, w_out, post_norm)


def _row(v):
    return v.reshape(1, -1).astype(F32)


def _pad_rows(w, rows, at):
    out = jnp.zeros((rows, w.shape[1]), w.dtype)
    return out.at[at:at + w.shape[0]].set(w)


def _layer_weights(l, w_in, rwkv_v_down, mla_w_uq, mla_w_ukv):
    o = np.cumsum((0, RWKV_SHIFT_COLS, RWKV_WIDTH, GMLP_WIDTH, GMLP_WIDTH, GMLP_WIDTH,
                   MLA_Q_LORA, MLA_KV_LORA, MLA_ROPE, MLA_WIDTH, N_BRANCHES * D_MODEL))
    w = w_in[l]
    seg = lambda i: w[:, int(o[i]):int(o[i + 1])]
    misc_a = jnp.zeros((D_MODEL, MISC), F32)
    if l > 0:
        misc_a = misc_a.at[:, 0:RWKV_VRES_LORA].set(rwkv_v_down[l - 1])
    misc_b = jnp.zeros((D_MODEL, MISC), F32).at[:, ROPE_LANE0:ROPE_LANE0 + MLA_ROPE].set(seg(7))
    w_all = jnp.concatenate([seg(0), seg(1), misc_a, seg(2), seg(3), seg(4),
                             seg(8), seg(5), seg(6), misc_b, seg(9)], axis=1).astype(BF16)
    qd = MLA_NOPE + MLA_ROPE
    wq = mla_w_uq[l].reshape(MLA_Q_LORA, MLA_HEADS, qd)
    wq = jnp.pad(wq, ((0, 0), (0, 0), (0, HEAD_PAD - qd))).reshape(MLA_Q_LORA, MLA_HEADS * HEAD_PAD)
    wkv = mla_w_ukv[l].reshape(MLA_KV_LORA, MLA_HEADS, MLA_NOPE + MLA_V)
    wk = jnp.pad(wkv[:, :, :MLA_NOPE], ((0, 0), (0, 0), (0, HEAD_PAD - MLA_NOPE)))
    wv = wkv[:, :, MLA_NOPE:].reshape(MLA_KV_LORA, MLA_HEADS // 2, 2, MLA_V)
    z = jnp.zeros_like(wv[:, :, 0])
    wv = jnp.stack([jnp.concatenate([wv[:, :, 0], z], axis=-1), jnp.concatenate([z, wv[:, :, 1]], axis=-1)], axis=2)
    wkv_all = jnp.concatenate([wk.reshape(MLA_KV_LORA, -1), wv.reshape(MLA_KV_LORA, -1)], axis=1)
    return w_all, wq.astype(BF16), wkv_all.astype(BF16)


def _constants(tc):
    head = np.arange(RWKV_WIDTH) // RWKV_HEAD_DIM
    hsum = (head[:, None] == head[None, :]).astype(np.float32)
    t = np.arange(tc)
    same = (t[:, None] // SUB) == (t[None, :] // SUB)
    cum = np.concatenate([same & (t[:, None] >= t[None, :]), same], axis=0).astype(np.float32)
    lane = np.arange(LANES)
    rope = (lane >= ROPE_LANE0) & (lane < ROPE_LANE0 + MLA_ROPE)
    return {"hsum": jnp.asarray(hsum, BF16), "cum": jnp.asarray(cum, BF16), "rope_lane": rope}


def kernel(x, positions, w_in, pre_norm, post_norm, rwkv_mu, rwkv_w0, rwkv_w_up, rwkv_a0, rwkv_a_up, rwkv_k_k, rwkv_k_a, rwkv_r_k, rwkv_ln_w, rwkv_ln_b, rwkv_v0, rwkv_v_down, rwkv_v_up, gmlp_ln_w, gmlp_ln_b, gmlp_w_s, gmlp_b_s, mla_q_norm, mla_w_uq, mla_kv_norm, mla_w_ukv, w_branch, w_out):
    bsz, seq, _ = x.shape
    n = bsz * seq
    depth = w_in.shape[0]
    tc = min(256, seq)
    tm = min(256, n)
    tq = min(256, seq)
    consts = _constants(tc)
    half = MLA_ROPE // 2
    inv_freq = 1.0 / (ROPE_THETA ** (jnp.arange(0, MLA_ROPE, 2, dtype=F32) / MLA_ROPE))
    inv_lane = jnp.where(consts["rope_lane"], jnp.tile(inv_freq, LANES // half), 0.0).reshape(1, LANES)
    pos128 = jnp.broadcast_to(positions.reshape(n, 1), (n, LANES))

    x2 = x.reshape(n, D_MODEL)
    v_first = None
    for l in range(depth):
        w_all, wq, wkv = _layer_weights(l, w_in, rwkv_v_down, mla_w_uq, mla_w_ukv)
        o_rw, o_g, o_mla, o_mg = _in_proj(x2, _row(pre_norm[l]), w_all, tm)

        rprm = {"mu": _row(rwkv_mu[l]), "w0": _row(rwkv_w0[l]), "a0": _row(rwkv_a0[l]),
                "w_up": _pad_rows(rwkv_w_up[l], LANES, 0).astype(BF16),
                "a_up": _pad_rows(rwkv_a_up[l], LANES, RWKV_DECAY_LORA).astype(BF16),
                "k_k": _row(rwkv_k_k[l]), "k_a": _row(rwkv_k_a[l]), "r_k": _row(rwkv_r_k[l]),
                "ln_w": _row(rwkv_ln_w[l]), "ln_b": _row(rwkv_ln_b[l])}
        if l > 0:
            rprm["v0"] = _row(rwkv_v0[l - 1])
            rprm["v_up"] = _pad_rows(rwkv_v_up[l - 1], MISC, 0).astype(BF16)
        y_a, v_l = _rwkv(o_rw, v_first, rprm, consts, bsz, seq, tc)
        if l == 0:
            v_first = v_l

        ws = gmlp_w_s[l].reshape(GMLP_GROUPS // 2, 2, GMLP_CHUNK, GMLP_CHUNK)
        gprm = {"ln_w": _row(gmlp_ln_w[l]), "ln_b": _row(gmlp_ln_b[l]),
                "w_s": jnp.concatenate([ws[:, 0], ws[:, 1]], axis=-1),
                "bias": jnp.repeat(gmlp_b_s[l].T, GMLP_WIDTH // GMLP_GROUPS, axis=1)}
        y_b = _gmlp(o_g, gprm, tm)

        mprm = {"inv_freq": inv_lane, "q_norm": _row(mla_q_norm[l]), "w_uq": wq,
                "kv_norm": _row(mla_kv_norm[l]), "w_ukv": wkv}
        q, k, vz = _mla_prep(o_mla, pos128, mprm, tm)
        y_c = _attn(q, k, vz, o_mla, bsz, seq, tq)

        x2 = _merge(y_a, y_b, y_c, o_mg, x2, w_branch[l].astype(BF16), w_out[l].astype(BF16),
                    _row(post_norm[l]), tm)
    return x2.reshape(bsz, seq, D_MODEL)
```

```python
import functools

import numpy as np
import jax
import jax.numpy as jnp
from jax import lax
from jax.experimental import pallas as pl
from jax.experimental.pallas import tpu as pltpu

F32 = jnp.float32
BF16 = jnp.bfloat16

D_MODEL = 1024
BRANCH_WIDTH = 512
N_BRANCHES = 3
NORM_EPS = 1e-6
RWKV_HEADS = 8
RWKV_HEAD_DIM = 64
RWKV_WIDTH = RWKV_HEADS * RWKV_HEAD_DIM
RWKV_DECAY_LORA = 64
RWKV_AAA_LORA = 64
RWKV_VRES_LORA = 32
RWKV_GN_EPS = 64e-5
RWKV_SHIFT_COLS = 3 * RWKV_WIDTH + RWKV_DECAY_LORA + RWKV_AAA_LORA
GMLP_WIDTH = 512
GMLP_GROUPS = 8
GMLP_CHUNK = 128
GMLP_LN_EPS = 1e-5
MLA_HEADS = 8
MLA_Q_LORA = 384
MLA_KV_LORA = 256
MLA_NOPE = 64
MLA_ROPE = 32
MLA_V = 64
MLA_WIDTH = MLA_HEADS * MLA_V
ROPE_THETA = 10000.0

LANES = 128
MISC = LANES
RW_COLS = RWKV_SHIFT_COLS + RWKV_WIDTH + MISC
G_COLS = 3 * GMLP_WIDTH
MLA_COLS = MLA_WIDTH + MLA_Q_LORA + MLA_KV_LORA + MISC
MG_COLS = N_BRANCHES * D_MODEL
ALL_COLS = RW_COLS + G_COLS + MLA_COLS + MG_COLS
HEAD_PAD = LANES
ROPE_LANE0 = MLA_NOPE
SUB = 64
PAIR = 2 * SUB

VMEM_LIMIT = 56 * 1024 * 1024


def _dot(a, b):
    return jnp.dot(a, b, preferred_element_type=F32)


def _dot_nt(a, b):
    return lax.dot_general(a, b, (((1,), (1,)), ((), ())), preferred_element_type=F32)


def _bdot(a, b):
    return lax.dot_general(a, b, (((2,), (1,)), ((0,), (0,))), preferred_element_type=F32)


def _bdot_nt(a, b):
    return lax.dot_general(a, b, (((2,), (2,)), ((0,), (0,))), preferred_element_type=F32)


def _bdot_tn(a, b):
    return lax.dot_general(a, b, (((1,), (1,)), ((0,), (0,))), preferred_element_type=F32)


def _split3(x):
    hi = x.astype(BF16)
    r1 = x - hi.astype(F32)
    mid = r1.astype(BF16)
    lo = (r1 - mid.astype(F32)).astype(BF16)
    return hi, mid, lo


def _dot_exact_lhs(m, x):
    hi, mid, lo = _split3(x)
    return _dot(m, hi) + _dot(m, mid) + _dot(m, lo)


def _head_sum(x, m):
    hi = x.astype(BF16)
    lo = (x - hi.astype(F32)).astype(BF16)
    return _dot(hi, m) + _dot(lo, m)


def _const_spec(shape):
    nd = len(shape)
    return pl.BlockSpec(shape, lambda *_: (0,) * nd)


def _in_proj_kernel(x_ref, g_ref, w_ref, o_rw, o_g, o_mla, o_mg):
    x = x_ref[...]
    ms = jnp.mean(x * x, axis=-1, keepdims=True)
    h = (x * lax.rsqrt(ms + NORM_EPS) * g_ref[...]).astype(BF16)
    col = 0
    for o in (o_rw, o_g, o_mla, o_mg):
        width = o.shape[1]
        for c0 in range(0, width, 512):
            cw = min(512, width - c0)
            o[:, c0:c0 + cw] = _dot(h, w_ref[:, col + c0:col + c0 + cw]).astype(o.dtype)
        col += width


def _in_proj(x2, gain, w_all, tm):
    n = x2.shape[0]
    outs = (RW_COLS, G_COLS, MLA_COLS, MG_COLS)
    return pl.pallas_call(
        _in_proj_kernel,
        grid=(n // tm,),
        in_specs=[pl.BlockSpec((tm, D_MODEL), lambda i: (i, 0)),
                  _const_spec((1, D_MODEL)),
                  pl.BlockSpec((D_MODEL, ALL_COLS), lambda i: (0, 0), pipeline_mode=pl.Buffered(1))],
        out_specs=[pl.BlockSpec((tm, c), lambda i: (i, 0)) for c in outs],
        out_shape=[jax.ShapeDtypeStruct((n, c), BF16) for c in outs],
        compiler_params=pltpu.CompilerParams(dimension_semantics=("arbitrary",), vmem_limit_bytes=VMEM_LIMIT),
        name="in_proj",
    )(x2, gain, w_all)


def _rwkv_kernel(has_vres, tc, *refs):
    it = iter(refs)
    rw_ref = next(it)
    vf_ref = next(it) if has_vres else None
    mu_ref, w0_ref, wup_ref, a0_ref, aup_ref, kk_ref, ka_ref, rk_ref, lnw_ref, lnb_ref = (next(it) for _ in range(10))
    if has_vres:
        v0_ref, vup_ref = next(it), next(it)
    hsum_ref, cum_ref = next(it), next(it)
    y_ref = next(it)
    vout_ref = None if has_vres else next(it)
    carry_ref, state_ref = next(it), next(it)
    at_s, rt_s, bt_s, kt_s, v_s, bd_s, kd_s = (next(it) for _ in range(7))
    pc_s, y_s = next(it), next(it)

    @pl.when(pl.program_id(1) == 0)
    def _():
        carry_ref[...] = jnp.zeros_like(carry_ref)
        state_ref[...] = jnp.zeros_like(state_ref)

    p = rw_ref[:, 0:RWKV_SHIFT_COLS].astype(F32)
    row = lax.broadcasted_iota(jnp.int32, p.shape, 0)
    prev = jnp.where(row == 0, carry_ref[0:1, :], pltpu.roll(p, 1, axis=0))
    carry_ref[0:1, :] = p[tc - 1:tc, :]
    p = p + (prev - p) * mu_ref[...]
    r = p[:, 0:RWKV_WIDTH]
    k = p[:, RWKV_WIDTH:2 * RWKV_WIDTH]
    v = p[:, 2 * RWKV_WIDTH:3 * RWKV_WIDTH]
    lora_in = p[:, 3 * RWKV_WIDTH:RWKV_SHIFT_COLS]

    z = -(w0_ref[...] + _dot(jnp.tanh(lora_in).astype(BF16), wup_ref[...]))
    softplus = jnp.maximum(z, 0.0) + jnp.log1p(jnp.exp(-jnp.abs(z)))
    lw = -jnp.exp(-softplus - 0.5)
    a = jax.nn.sigmoid(a0_ref[...] + _dot(lora_in.astype(BF16), aup_ref[...]))
    if has_vres:
        vd = rw_ref[:, RWKV_SHIFT_COLS + RWKV_WIDTH:RW_COLS]
        vg = jax.nn.sigmoid(v0_ref[...] + _dot(vd, vup_ref[...]))
        v = v + (vf_ref[...] - v) * vg
    else:
        vout_ref[...] = v

    hsum = hsum_ref[...]
    kk = k * kk_ref[...]
    kk = kk * lax.rsqrt(jnp.maximum(_head_sum(kk * kk, hsum), 1e-24))
    k2 = k * (1.0 + (a - 1.0) * ka_ref[...])
    av = -kk
    bv = kk * a

    gs = _dot_exact_lhs(cum_ref[...], lw)
    g = gs[0:tc]
    gl = gs[tc:2 * tc]
    e_neg = jnp.exp(-g)
    e_end = jnp.exp(gl - g)
    lane = lax.broadcasted_iota(jnp.int32, (tc, RWKV_WIDTH), 1)
    first16 = jnp.where((lane % LANES) < RWKV_HEAD_DIM, 1.0, 0.0).astype(BF16)
    nsub = tc // SUB
    npair = RWKV_HEADS // 2

    def pieces(x):
        for c in range(nsub):
            for j in range(npair):
                yield c * npair + j, x[SUB * c:SUB * (c + 1), LANES * j:LANES * (j + 1)]

    def put(ref, x):
        x16 = x.astype(BF16)
        xa = x16 * first16
        xb = x16 - xa
        for b, piece in pieces(xa):
            ref[b, 0:SUB, :] = piece
        for b, piece in pieces(xb):
            ref[b, SUB:PAIR, :] = piece

    put(at_s, av * jnp.exp(g - lw))
    put(rt_s, r * jnp.exp(g))
    put(bt_s, bv * e_neg)
    put(kt_s, k2 * e_neg)
    put(v_s, v)
    put(bd_s, bv * e_end)
    put(kd_s, k2 * e_end)
    for b, piece in pieces(jnp.exp(gl)):
        pc_s[b, 0:SUB, :] = piece
        pc_s[b, SUB:PAIR, :] = piece

    ri = lax.broadcasted_iota(jnp.int32, (PAIR, PAIR), 0)
    ci = lax.broadcasted_iota(jnp.int32, (PAIR, PAIR), 1)
    strict = ri > ci
    incl = ri >= ci
    eye = jnp.where(ri == ci, 1.0, 0.0).astype(F32)
    levels = (4, 8, 16, 32, 64)
    lvl_mask = [strict & ((ri // b) == (ci // b)) & ((ri // (b // 2)) != (ci // (b // 2))) for b in levels]
    lvl2 = strict & ((ri // 2) == (ci // 2))

    at, rt, bt, kt, vv, bd, kd = (s[...] for s in (at_s, rt_s, bt_s, kt_s, v_s, bd_s, kd_s))
    s = _bdot_nt(jnp.concatenate([at, rt], axis=1), jnp.concatenate([bt, kt], axis=1))
    nab = jnp.where(strict, s[:, 0:PAIR, 0:PAIR], 0.0)
    aak = jnp.where(strict, s[:, 0:PAIR, PAIR:2 * PAIR], 0.0).astype(BF16)
    rb = jnp.where(incl, s[:, PAIR:2 * PAIR, 0:PAIR], 0.0).astype(BF16)
    rk = jnp.where(incl, s[:, PAIR:2 * PAIR, PAIR:2 * PAIR], 0.0).astype(BF16)
    d = eye + jnp.where(lvl2, nab, 0.0)
    for m in lvl_mask:
        e = jnp.where(m, nab, 0.0).astype(BF16)
        db = d.astype(BF16)
        d = d + _bdot(_bdot(db, e).astype(BF16), db)
    t = d.astype(BF16)
    w1 = _bdot(aak, vv).astype(BF16)
    au = _bdot(t, jnp.concatenate([at, w1], axis=2))
    ah = au[:, :, 0:PAIR].astype(BF16)
    uh = au[:, :, PAIR:2 * PAIR]
    kv = _bdot_tn(kd, vv)
    pcm = jnp.swapaxes(pc_s[...], 1, 2)
    rbk = jnp.concatenate([rb, rk], axis=2)

    h = state_ref[...]
    for c in range(nsub):
        sl = slice(c * npair, (c + 1) * npair)
        arh = _bdot(jnp.concatenate([ah[sl], rt[sl]], axis=1), h.astype(BF16))
        u = (arh[:, 0:PAIR] + uh[sl]).astype(BF16)
        h = pcm[sl] * h + _bdot_tn(bd[sl], u) + kv[sl]
        y = arh[:, PAIR:2 * PAIR] + _bdot(rbk[sl], jnp.concatenate([u, vv[sl]], axis=1))
        for j in range(npair):
            y_s[SUB * c:SUB * (c + 1), LANES * j:LANES * (j + 1)] = y[j, 0:SUB] + y[j, SUB:PAIR]
    state_ref[...] = h

    y = y_s[...]
    inv_n = 1.0 / RWKV_HEAD_DIM
    mean = _head_sum(y, hsum) * inv_n
    yc = y - mean
    var = _head_sum(yc * yc, hsum) * inv_n
    y = yc * lax.rsqrt(var + RWKV_GN_EPS) * lnw_ref[...] + lnb_ref[...]
    y = y + _head_sum(r * k2 * rk_ref[...], hsum) * v
    gate = rw_ref[:, RWKV_SHIFT_COLS:RWKV_SHIFT_COLS + RWKV_WIDTH].astype(F32)
    y_ref[...] = (y * gate * jax.nn.sigmoid(gate)).astype(y_ref.dtype)


def _rwkv(o_rw, v_first, prm, consts, bsz, seq, tc):
    has_vres = v_first is not None
    n = bsz * seq
    nt = seq // tc
    row = lambda b, t: (b * nt + t, 0)
    args = [o_rw]
    specs = [pl.BlockSpec((tc, RW_COLS), row)]
    if has_vres:
        args.append(v_first)
        specs.append(pl.BlockSpec((tc, RWKV_WIDTH), row))
    names = ["mu", "w0", "w_up", "a0", "a_up", "k_k", "k_a", "r_k", "ln_w", "ln_b"]
    if has_vres:
        names += ["v0", "v_up"]
    for nm in names:
        args.append(prm[nm])
        specs.append(_const_spec(prm[nm].shape))
    for c in (consts["hsum"], consts["cum"]):
        args.append(c)
        specs.append(_const_spec(c.shape))
    out_shape = [jax.ShapeDtypeStruct((n, RWKV_WIDTH), BF16)]
    out_specs = [pl.BlockSpec((tc, RWKV_WIDTH), row)]
    if not has_vres:
        out_shape.append(jax.ShapeDtypeStruct((n, RWKV_WIDTH), F32))
        out_specs.append(pl.BlockSpec((tc, RWKV_WIDTH), row))
    scratch = [pltpu.VMEM((8, RWKV_SHIFT_COLS), F32),
               pltpu.VMEM((RWKV_HEADS // 2, PAIR, PAIR), F32)]
    nb = (tc // SUB) * (RWKV_HEADS // 2)
    scratch += [pltpu.VMEM((nb, PAIR, PAIR), BF16) for _ in range(7)]
    scratch += [pltpu.VMEM((nb, PAIR, PAIR), F32), pltpu.VMEM((tc, RWKV_WIDTH), F32)]
    res = pl.pallas_call(
        functools.partial(_rwkv_kernel, has_vres, tc),
        grid=(bsz, nt),
        in_specs=specs,
        out_specs=out_specs,
        out_shape=out_shape,
        scratch_shapes=scratch,
        compiler_params=pltpu.CompilerParams(dimension_semantics=("arbitrary", "arbitrary"),
                                             vmem_limit_bytes=VMEM_LIMIT),
        name="rwkv_vres" if has_vres else "rwkv",
    )(*args)
    return (res[0], v_first) if has_vres else (res[0], res[1])


def _gelu(x):
    return 0.5 * x * (1.0 + lax.erf(x * np.float32(1.0 / np.sqrt(2.0))))


def _gmlp_kernel(tm, g_ref, lnw_ref, lnb_ref, ws_ref, bias_ref, o_ref):
    u = _gelu(g_ref[:, 0:GMLP_WIDTH].astype(F32))
    v = _gelu(g_ref[:, GMLP_WIDTH:2 * GMLP_WIDTH].astype(F32))
    mu = jnp.mean(v, axis=-1, keepdims=True)
    vc = v - mu
    var = jnp.mean(vc * vc, axis=-1, keepdims=True)
    v = vc * lax.rsqrt(var + GMLP_LN_EPS) * lnw_ref[...] + lnb_ref[...]
    gate = g_ref[:, 2 * GMLP_WIDTH:3 * GMLP_WIDTH].astype(F32)
    ug = u * gate * jax.nn.sigmoid(gate)
    lane = lax.broadcasted_iota(jnp.int32, (GMLP_CHUNK, LANES), 1)
    first = lane < (LANES // 2)
    ri = lax.broadcasted_iota(jnp.int32, (GMLP_CHUNK, 2 * GMLP_CHUNK), 0)
    ci = lax.broadcasted_iota(jnp.int32, (GMLP_CHUNK, 2 * GMLP_CHUNK), 1)
    causal = ri >= (ci % GMLP_CHUNK)
    for j in range(GMLP_GROUPS // 2):
        w = jnp.where(causal, ws_ref[j], 0.0).astype(BF16)
        ls = slice(LANES * j, LANES * (j + 1))
        for c in range(tm // GMLP_CHUNK):
            rs = slice(GMLP_CHUNK * c, GMLP_CHUNK * (c + 1))
            vp = v[rs, ls]
            stacked = jnp.concatenate([jnp.where(first, vp, 0.0), jnp.where(first, 0.0, vp)], axis=0).astype(BF16)
            vm = _dot(w, stacked) + bias_ref[:, ls]
            o_ref[rs, ls] = (ug[rs, ls] * vm).astype(o_ref.dtype)


def _gmlp(o_g, prm, tm):
    n = o_g.shape[0]
    return pl.pallas_call(
        functools.partial(_gmlp_kernel, tm),
        grid=(n // tm,),
        in_specs=[pl.BlockSpec((tm, G_COLS), lambda i: (i, 0)),
                  _const_spec((1, GMLP_WIDTH)), _const_spec((1, GMLP_WIDTH)),
                  _const_spec(prm["w_s"].shape), _const_spec(prm["bias"].shape)],
        out_specs=pl.BlockSpec((tm, GMLP_WIDTH), lambda i: (i, 0)),
        out_shape=jax.ShapeDtypeStruct((n, GMLP_WIDTH), BF16),
        compiler_params=pltpu.CompilerParams(dimension_semantics=("arbitrary",), vmem_limit_bytes=VMEM_LIMIT),
        name="gmlp",
    )(o_g, prm["ln_w"], prm["ln_b"], prm["w_s"], prm["bias"])


def _rms(x, gain):
    return x * lax.rsqrt(jnp.mean(x * x, axis=-1, keepdims=True) + NORM_EPS) * gain


def _mla_prep_kernel(m_ref, pos_ref, invf_ref, qn_ref, wq_ref, kvn_ref, wkv_ref, q_ref, k_ref, v_ref):
    c0 = MLA_WIDTH
    c_q = m_ref[:, c0:c0 + MLA_Q_LORA].astype(F32)
    c_kv = m_ref[:, c0 + MLA_Q_LORA:c0 + MLA_Q_LORA + MLA_KV_LORA].astype(F32)
    k_rope = m_ref[:, c0 + MLA_Q_LORA + MLA_KV_LORA:MLA_COLS].astype(F32)
    ang = pos_ref[...].astype(F32) * invf_ref[...]
    cos = jnp.cos(ang)
    sin = jnp.sin(ang)
    lane = lax.broadcasted_iota(jnp.int32, ang.shape, 1)
    half = MLA_ROPE // 2
    sin_hi = jnp.where((lane >= ROPE_LANE0 + half) & (lane < ROPE_LANE0 + MLA_ROPE), sin, 0.0)
    sin_lo = jnp.where((lane >= ROPE_LANE0) & (lane < ROPE_LANE0 + half), -sin, 0.0)

    def rope(x):
        return x * cos + pltpu.roll(x, half, axis=1) * sin_hi + pltpu.roll(x, LANES - half, axis=1) * sin_lo

    scale = np.float32((MLA_NOPE + MLA_ROPE) ** -0.5)
    q = _dot(_rms(c_q, qn_ref[...]).astype(BF16), wq_ref[...])
    kv = _dot(_rms(c_kv, kvn_ref[...]).astype(BF16), wkv_ref[...])
    kr = rope(k_rope)
    for h in range(MLA_HEADS):
        ls = slice(HEAD_PAD * h, HEAD_PAD * (h + 1))
        q_ref[:, ls] = (rope(q[:, ls]) * scale).astype(BF16)
        k_ref[:, ls] = (kv[:, ls] + kr).astype(BF16)
    v_ref[...] = kv[:, MLA_HEADS * HEAD_PAD:].astype(BF16)


def _mla_prep(o_mla, pos128, prm, tm):
    n = o_mla.shape[0]
    wide = MLA_HEADS * HEAD_PAD
    return pl.pallas_call(
        _mla_prep_kernel,
        grid=(n // tm,),
        in_specs=[pl.BlockSpec((tm, MLA_COLS), lambda i: (i, 0)),
                  pl.BlockSpec((tm, LANES), lambda i: (i, 0)),
                  _const_spec((1, LANES)),
                  _const_spec((1, MLA_Q_LORA)), _const_spec(prm["w_uq"].shape),
                  _const_spec((1, MLA_KV_LORA)), _const_spec(prm["w_ukv"].shape)],
        out_specs=[pl.BlockSpec((tm, wide), lambda i: (i, 0))] * 3,
        out_shape=[jax.ShapeDtypeStruct((n, wide), BF16)] * 3,
        compiler_params=pltpu.CompilerParams(dimension_semantics=("arbitrary",), vmem_limit_bytes=VMEM_LIMIT),
        name="mla_prep",
    )(o_mla, pos128, prm["inv_freq"], prm["q_norm"], prm["w_uq"], prm["kv_norm"], prm["w_ukv"])


def _attn_kernel(tq, q_ref, k_ref, v_ref, gate_ref, o_ref):
    i = pl.program_id(2)
    ri = lax.broadcasted_iota(jnp.int32, (tq, tq), 0)
    ci = lax.broadcasted_iota(jnp.int32, (tq, tq), 1)
    causal = ci <= ri
    heads = [slice(HEAD_PAD * hh, HEAD_PAD * (hh + 1)) for hh in range(2)]

    def block(j, carry, masked):
        r0 = pl.multiple_of(j * tq, tq)
        new = []
        for ls, (m, l, acc) in zip(heads, carry):
            s = _dot_nt(q_ref[:, ls], k_ref[pl.ds(r0, tq), ls])
            if masked:
                s = jnp.where(causal, s, -jnp.inf)
            m_new = jnp.maximum(m, jnp.max(s, axis=-1, keepdims=True))
            alpha = jnp.exp(m - m_new)
            p = jnp.exp(s - m_new)
            l = alpha * l + jnp.sum(p, axis=-1, keepdims=True)
            acc = alpha * acc + _dot(p.astype(BF16), v_ref[pl.ds(r0, tq), ls])
            new.append((m_new, l, acc))
        return tuple(new)

    init = (jnp.full((tq, 1), -jnp.inf, F32), jnp.zeros((tq, 1), F32), jnp.zeros((tq, LANES), F32))
    carry = lax.fori_loop(0, i, functools.partial(block, masked=False), (init, init))
    carry = block(i, carry, True)
    out = sum(acc * (1.0 / l) for _, l, acc in carry)
    gate = gate_ref[...].astype(F32)
    o_ref[...] = (out * gate * jax.nn.sigmoid(gate)).astype(o_ref.dtype)


def _attn(q, k, vz, o_mla, bsz, seq, tq):
    n = bsz * seq
    nq = seq // tq
    pairs = MLA_HEADS // 2
    return pl.pallas_call(
        functools.partial(_attn_kernel, tq),
        grid=(bsz, pairs, nq),
        in_specs=[pl.BlockSpec((tq, 2 * HEAD_PAD), lambda b, h, i: (b * nq + i, h)),
                  pl.BlockSpec((seq, 2 * HEAD_PAD), lambda b, h, i: (b, h)),
                  pl.BlockSpec((seq, 2 * HEAD_PAD), lambda b, h, i: (b, h)),
                  pl.BlockSpec((tq, LANES), lambda b, h, i: (b * nq + i, h))],
        out_specs=pl.BlockSpec((tq, LANES), lambda b, h, i: (b * nq + i, h)),
        out_shape=jax.ShapeDtypeStruct((n, MLA_WIDTH), BF16),
        compiler_params=pltpu.CompilerParams(dimension_semantics=("arbitrary",) * 3, vmem_limit_bytes=VMEM_LIMIT),
        name="attn",
    )(q, k, vz, o_mla)


def _merge_kernel(ya_ref, yb_ref, yc_ref, mg_ref, x_ref, wb_ref, wo_ref, pn_ref, o_ref):
    ys = (ya_ref[...], yb_ref[...], yc_ref[...])
    halves = []
    for c in range(D_MODEL // 512):
        acc = None
        for nb in range(N_BRANCHES):
            cols = slice(nb * D_MODEL + c * 512, nb * D_MODEL + (c + 1) * 512)
            term = jax.nn.sigmoid(mg_ref[:, cols].astype(F32)) * _dot(ys[nb], wb_ref[nb, :, c * 512:(c + 1) * 512])
            acc = term if acc is None else acc + term
        halves.append(acc.astype(BF16))
    mixed = _dot(jnp.concatenate(halves, axis=1), wo_ref[...])
    ms = jnp.mean(mixed * mixed, axis=-1, keepdims=True)
    o_ref[...] = x_ref[...] + mixed * lax.rsqrt(ms + NORM_EPS) * pn_ref[...]


def _merge(ya, yb, yc, o_mg, x2, w_branch, w_out, post_norm, tm):
    n = x2.shape[0]
    rowspec = lambda c: pl.BlockSpec((tm, c), lambda i: (i, 0))
    return pl.pallas_call(
        _merge_kernel,
        grid=(n // tm,),
        in_specs=[rowspec(BRANCH_WIDTH), rowspec(BRANCH_WIDTH), rowspec(BRANCH_WIDTH), rowspec(MG_COLS),
                  rowspec(D_MODEL), _const_spec(w_branch.shape), _const_spec(w_out.shape), _const_spec((1, D_MODEL))],
        out_specs=rowspec(D_MODEL),
        out_shape=jax.ShapeDtypeStruct((n, D_MODEL), F32),
        compiler_params=pltpu.CompilerParams(dimension_semantics=("arbitrary",), vmem_limit_bytes=VMEM_LIMIT),
        name="merge",
    )(ya, yb, yc, o_mg, x2, w_branch, w_out, post_norm)


def _row(v):
    return v.reshape(1, -1).astype(F32)


def _pad_rows(w, rows, at):
    out = jnp.zeros((rows, w.shape[1]), w.dtype)
    return out.at[at:at + w.shape[0]].set(w)


def _layer_weights(l, w_in, rwkv_v_down, mla_w_uq, mla_w_ukv):
    o = np.cumsum((0, RWKV_SHIFT_COLS, RWKV_WIDTH, GMLP_WIDTH, GMLP_WIDTH, GMLP_WIDTH,
                   MLA_Q_LORA, MLA_KV_LORA, MLA_ROPE, MLA_WIDTH, N_BRANCHES * D_MODEL))
    w = w_in[l]
    seg = lambda i: w[:, int(o[i]):int(o[i + 1])]
    misc_a = jnp.zeros((D_MODEL, MISC), F32)
    if l > 0:
        misc_a = misc_a.at[:, 0:RWKV_VRES_LORA].set(rwkv_v_down[l - 1])
    misc_b = jnp.zeros((D_MODEL, MISC), F32).at[:, ROPE_LANE0:ROPE_LANE0 + MLA_ROPE].set(seg(7))
    w_all = jnp.concatenate([seg(0), seg(1), misc_a, seg(2), seg(3), seg(4),
                             seg(8), seg(5), seg(6), misc_b, seg(9)], axis=1).astype(BF16)
    qd = MLA_NOPE + MLA_ROPE
    wq = mla_w_uq[l].reshape(MLA_Q_LORA, MLA_HEADS, qd)
    wq = jnp.pad(wq, ((0, 0), (0, 0), (0, HEAD_PAD - qd))).reshape(MLA_Q_LORA, MLA_HEADS * HEAD_PAD)
    wkv = mla_w_ukv[l].reshape(MLA_KV_LORA, MLA_HEADS, MLA_NOPE + MLA_V)
    wk = jnp.pad(wkv[:, :, :MLA_NOPE], ((0, 0), (0, 0), (0, HEAD_PAD - MLA_NOPE)))
    wv = wkv[:, :, MLA_NOPE:].reshape(MLA_KV_LORA, MLA_HEADS // 2, 2, MLA_V)
    z = jnp.zeros_like(wv[:, :, 0])
    wv = jnp.stack([jnp.concatenate([wv[:, :, 0], z], axis=-1), jnp.concatenate([z, wv[:, :, 1]], axis=-1)], axis=2)
    wkv_all = jnp.concatenate([wk.reshape(MLA_KV_LORA, -1), wv.reshape(MLA_KV_LORA, -1)], axis=1)
    return w_all, wq.astype(BF16), wkv_all.astype(BF16)


def _constants(tc):
    head = np.arange(RWKV_WIDTH) // RWKV_HEAD_DIM
    hsum = (head[:, None] == head[None, :]).astype(np.float32)
    t = np.arange(tc)
    same = (t[:, None] // SUB) == (t[None, :] // SUB)
    cum = np.concatenate([same & (t[:, None] >= t[None, :]), same], axis=0).astype(np.float32)
    lane = np.arange(LANES)
    rope = (lane >= ROPE_LANE0) & (lane < ROPE_LANE0 + MLA_ROPE)
    return {"hsum": jnp.asarray(hsum, BF16), "cum": jnp.asarray(cum, BF16), "rope_lane": rope}


def kernel(x, positions, w_in, pre_norm, post_norm, rwkv_mu, rwkv_w0, rwkv_w_up, rwkv_a0, rwkv_a_up, rwkv_k_k, rwkv_k_a, rwkv_r_k, rwkv_ln_w, rwkv_ln_b, rwkv_v0, rwkv_v_down, rwkv_v_up, gmlp_ln_w, gmlp_ln_b, gmlp_w_s, gmlp_b_s, mla_q_norm, mla_w_uq, mla_kv_norm, mla_w_ukv, w_branch, w_out):
    bsz, seq, _ = x.shape
    n = bsz * seq
    depth = w_in.shape[0]
    tc = min(256, seq)
    tm = min(256, n)
    tq = min(512, seq)
    consts = _constants(tc)
    half = MLA_ROPE // 2
    inv_freq = 1.0 / (ROPE_THETA ** (jnp.arange(0, MLA_ROPE, 2, dtype=F32) / MLA_ROPE))
    inv_lane = jnp.where(consts["rope_lane"], jnp.tile(inv_freq, LANES // half), 0.0).reshape(1, LANES)
    pos128 = jnp.broadcast_to(positions.reshape(n, 1), (n, LANES))

    x2 = x.reshape(n, D_MODEL)
    v_first = None
    for l in range(depth):
        w_all, wq, wkv = _layer_weights(l, w_in, rwkv_v_down, mla_w_uq, mla_w_ukv)
        o_rw, o_g, o_mla, o_mg = _in_proj(x2, _row(pre_norm[l]), w_all, tm)

        rprm = {"mu": _row(rwkv_mu[l]), "w0": _row(rwkv_w0[l]), "a0": _row(rwkv_a0[l]),
                "w_up": _pad_rows(rwkv_w_up[l], LANES, 0).astype(BF16),
                "a_up": _pad_rows(rwkv_a_up[l], LANES, RWKV_DECAY_LORA).astype(BF16),
                "k_k": _row(rwkv_k_k[l]), "k_a": _row(rwkv_k_a[l]), "r_k": _row(rwkv_r_k[l]),
                "ln_w": _row(rwkv_ln_w[l]), "ln_b": _row(rwkv_ln_b[l])}
        if l > 0:
            rprm["v0"] = _row(rwkv_v0[l - 1])
            rprm["v_up"] = _pad_rows(rwkv_v_up[l - 1], MISC, 0).astype(BF16)
        y_a, v_l = _rwkv(o_rw, v_first, rprm, consts, bsz, seq, tc)
        if l == 0:
            v_first = v_l

        ws = gmlp_w_s[l].reshape(GMLP_GROUPS // 2, 2, GMLP_CHUNK, GMLP_CHUNK)
        gprm = {"ln_w": _row(gmlp_ln_w[l]), "ln_b": _row(gmlp_ln_b[l]),
                "w_s": jnp.concatenate([ws[:, 0], ws[:, 1]], axis=-1),
                "bias": jnp.repeat(gmlp_b_s[l].T, GMLP_WIDTH // GMLP_GROUPS, axis=1)}
        y_b = _gmlp(o_g, gprm, tm)

        mprm = {"inv_freq": inv_lane, "q_norm": _row(mla_q_norm[l]), "w_uq": wq,
                "kv_norm": _row(mla_kv_norm[l]), "w_ukv": wkv}
        q, k, vz = _mla_prep(o_mla, pos128, mprm, tm)
        y_c = _attn(q, k, vz, o_mla, bsz, seq, tq)

        x2 = _merge(y_a, y_b, y_c, o_mg, x2, w_branch[l].astype(BF16), w_out[l].astype(BF16),
                    _row(post_norm[l]), tm)
    return x2.reshape(bsz, seq, D_MODEL)
```

```python
import functools

import numpy as np
import jax
import jax.numpy as jnp
from jax import lax
from jax.experimental import pallas as pl
from jax.experimental.pallas import tpu as pltpu

F32 = jnp.float32
BF16 = jnp.bfloat16

D_MODEL = 1024
BRANCH_WIDTH = 512
N_BRANCHES = 3
NORM_EPS = 1e-6
RWKV_HEADS = 8
RWKV_HEAD_DIM = 64
RWKV_WIDTH = RWKV_HEADS * RWKV_HEAD_DIM
RWKV_DECAY_LORA = 64
RWKV_AAA_LORA = 64
RWKV_VRES_LORA = 32
RWKV_GN_EPS = 64e-5
RWKV_SHIFT_COLS = 3 * RWKV_WIDTH + RWKV_DECAY_LORA + RWKV_AAA_LORA
GMLP_WIDTH = 512
GMLP_GROUPS = 8
GMLP_CHUNK = 128
GMLP_LN_EPS = 1e-5
MLA_HEADS = 8
MLA_Q_LORA = 384
MLA_KV_LORA = 256
MLA_NOPE = 64
MLA_ROPE = 32
MLA_V = 64
MLA_WIDTH = MLA_HEADS * MLA_V
ROPE_THETA = 10000.0

LANES = 128
MISC = LANES
RW_COLS = RWKV_SHIFT_COLS + RWKV_WIDTH + MISC
G_COLS = 3 * GMLP_WIDTH
MLA_COLS = MLA_WIDTH + MLA_Q_LORA + MLA_KV_LORA + MISC
MG_COLS = N_BRANCHES * D_MODEL
ALL_COLS = RW_COLS + G_COLS + MLA_COLS + MG_COLS
HEAD_PAD = LANES
ROPE_LANE0 = MLA_NOPE
SUB = 64
PAIR = 2 * SUB

VMEM_LIMIT = 56 * 1024 * 1024


def _dot(a, b):
    return jnp.dot(a, b, preferred_element_type=F32)


def _dot_nt(a, b):
    return lax.dot_general(a, b, (((1,), (1,)), ((), ())), preferred_element_type=F32)


def _bdot(a, b):
    return lax.dot_general(a, b, (((2,), (1,)), ((0,), (0,))), preferred_element_type=F32)


def _bdot_nt(a, b):
    return lax.dot_general(a, b, (((2,), (2,)), ((0,), (0,))), preferred_element_type=F32)


def _bdot_tn(a, b):
    return lax.dot_general(a, b, (((1,), (1,)), ((0,), (0,))), preferred_element_type=F32)


def _split3(x):
    hi = x.astype(BF16)
    r1 = x - hi.astype(F32)
    mid = r1.astype(BF16)
    lo = (r1 - mid.astype(F32)).astype(BF16)
    return hi, mid, lo


def _dot_split2_lhs(m, x):
    hi = x.astype(BF16)
    lo = (x - hi.astype(F32)).astype(BF16)
    return _dot(m, hi) + _dot(m, lo)


def _dot_exact_rhs(x, m):
    hi, mid, lo = _split3(x)
    return _dot(hi, m) + _dot(mid, m) + _dot(lo, m)


def _head_sum(x, m):
    x16 = x.astype(BF16)
    return jnp.concatenate([_dot(x16[:, c:c + LANES], m) for c in range(0, x.shape[1], LANES)], axis=1)


def _const_spec(shape):
    nd = len(shape)
    return pl.BlockSpec(shape, lambda *_: (0,) * nd)


def _in_proj_kernel(x_ref, g_ref, w_ref, o_rw, o_g, o_mla, o_mg):
    x = x_ref[...]
    ms = jnp.mean(x * x, axis=-1, keepdims=True)
    h = (x * lax.rsqrt(ms + NORM_EPS) * g_ref[...]).astype(BF16)
    col = 0
    for o in (o_rw, o_g, o_mla, o_mg):
        width = o.shape[1]
        for c0 in range(0, width, 512):
            cw = min(512, width - c0)
            o[:, c0:c0 + cw] = _dot(h, w_ref[:, col + c0:col + c0 + cw]).astype(o.dtype)
        col += width


def _in_proj(x2, gain, w_all, tm):
    n = x2.shape[0]
    outs = (RW_COLS, G_COLS, MLA_COLS, MG_COLS)
    return pl.pallas_call(
        _in_proj_kernel,
        grid=(n // tm,),
        in_specs=[pl.BlockSpec((tm, D_MODEL), lambda i: (i, 0)),
                  _const_spec((1, D_MODEL)),
                  pl.BlockSpec((D_MODEL, ALL_COLS), lambda i: (0, 0), pipeline_mode=pl.Buffered(1))],
        out_specs=[pl.BlockSpec((tm, c), lambda i: (i, 0)) for c in outs],
        out_shape=[jax.ShapeDtypeStruct((n, c), BF16) for c in outs],
        compiler_params=pltpu.CompilerParams(dimension_semantics=("arbitrary",), vmem_limit_bytes=VMEM_LIMIT),
        name="in_proj",
    )(x2, gain, w_all)


def _rwkv_kernel(has_vres, tc, *refs):
    it = iter(refs)
    rw_ref = next(it)
    vf_ref = next(it) if has_vres else None
    mu_ref, w0_ref, wup_ref, a0_ref, aup_ref, kk_ref, ka_ref, rk_ref, lnw_ref, lnb_ref = (next(it) for _ in range(10))
    if has_vres:
        v0_ref, vup_ref = next(it), next(it)
    hsum_ref, cum_ref = next(it), next(it)
    y_ref = next(it)
    vout_ref = None if has_vres else next(it)
    carry_ref, state_ref = next(it), next(it)
    at_s, rt_s, bt_s, kt_s, v_s, bd_s, kd_s = (next(it) for _ in range(7))
    pc_s, y_s = next(it), next(it)

    @pl.when(pl.program_id(1) == 0)
    def _():
        carry_ref[...] = jnp.zeros_like(carry_ref)
        state_ref[...] = jnp.zeros_like(state_ref)

    p = rw_ref[:, 0:RWKV_SHIFT_COLS].astype(F32)
    row = lax.broadcasted_iota(jnp.int32, p.shape, 0)
    prev = jnp.where(row == 0, carry_ref[0:1, :], pltpu.roll(p, 1, axis=0))
    carry_ref[0:1, :] = p[tc - 1:tc, :]
    p = p + (prev - p) * mu_ref[...]
    r = p[:, 0:RWKV_WIDTH]
    k = p[:, RWKV_WIDTH:2 * RWKV_WIDTH]
    v = p[:, 2 * RWKV_WIDTH:3 * RWKV_WIDTH]
    lora_in = p[:, 3 * RWKV_WIDTH:RWKV_SHIFT_COLS]

    z = -(w0_ref[...] + _dot(jnp.tanh(lora_in).astype(BF16), wup_ref[...]))
    softplus = jnp.maximum(z, 0.0) + jnp.log1p(jnp.exp(-jnp.abs(z)))
    lw = -jnp.exp(-softplus - 0.5)
    a = jax.nn.sigmoid(a0_ref[...] + _dot(lora_in.astype(BF16), aup_ref[...]))
    if has_vres:
        vd = rw_ref[:, RWKV_SHIFT_COLS + RWKV_WIDTH:RW_COLS]
        vg = jax.nn.sigmoid(v0_ref[...] + _dot(vd, vup_ref[...]))
        v = v + (vf_ref[...] - v) * vg
    else:
        vout_ref[...] = v

    hsum = hsum_ref[...]
    kk = k * kk_ref[...]
    kk = kk * lax.rsqrt(jnp.maximum(_head_sum(kk * kk, hsum), 1e-24))
    k2 = k * (1.0 + (a - 1.0) * ka_ref[...])
    av = -kk
    bv = kk * a

    gs = _dot_split2_lhs(cum_ref[...], lw)
    g = gs[0:tc]
    gl = gs[tc:2 * tc]
    e_neg = jnp.exp(-g)
    e_end = jnp.exp(gl - g)
    lane = lax.broadcasted_iota(jnp.int32, (tc, RWKV_WIDTH), 1)
    first16 = jnp.where((lane % LANES) < RWKV_HEAD_DIM, 1.0, 0.0).astype(BF16)
    nsub = tc // SUB
    npair = RWKV_HEADS // 2

    def pieces(x):
        for c in range(nsub):
            for j in range(npair):
                yield c * npair + j, x[SUB * c:SUB * (c + 1), LANES * j:LANES * (j + 1)]

    def put(ref, x):
        x16 = x.astype(BF16)
        xa = x16 * first16
        xb = x16 - xa
        for b, piece in pieces(xa):
            ref[b, 0:SUB, :] = piece
        for b, piece in pieces(xb):
            ref[b, SUB:PAIR, :] = piece

    put(at_s, av * jnp.exp(g - lw))
    put(rt_s, r * jnp.exp(g))
    put(bt_s, bv * e_neg)
    put(kt_s, k2 * e_neg)
    put(v_s, v)
    put(bd_s, bv * e_end)
    put(kd_s, k2 * e_end)
    for b, piece in pieces(jnp.exp(gl)):
        pc_s[b, 0:SUB, :] = piece
        pc_s[b, SUB:PAIR, :] = piece

    ri = lax.broadcasted_iota(jnp.int32, (PAIR, PAIR), 0)
    ci = lax.broadcasted_iota(jnp.int32, (PAIR, PAIR), 1)
    strict = ri > ci
    incl = ri >= ci
    eye = jnp.where(ri == ci, 1.0, 0.0).astype(F32)
    levels = (4, 8, 16, 32, 64)
    lvl_mask = [strict & ((ri // b) == (ci // b)) & ((ri // (b // 2)) != (ci // (b // 2))) for b in levels]
    lvl2 = strict & ((ri // 2) == (ci // 2))

    at, rt, bt, kt, vv, bd, kd = (s[...] for s in (at_s, rt_s, bt_s, kt_s, v_s, bd_s, kd_s))
    s = _bdot_nt(jnp.concatenate([at, rt], axis=1), jnp.concatenate([bt, kt], axis=1))
    nab = jnp.where(strict, s[:, 0:PAIR, 0:PAIR], 0.0)
    aak = jnp.where(strict, s[:, 0:PAIR, PAIR:2 * PAIR], 0.0).astype(BF16)
    rb = jnp.where(incl, s[:, PAIR:2 * PAIR, 0:PAIR], 0.0).astype(BF16)
    rk = jnp.where(incl, s[:, PAIR:2 * PAIR, PAIR:2 * PAIR], 0.0).astype(BF16)
    d = eye + jnp.where(lvl2, nab, 0.0)
    for m in lvl_mask:
        e = jnp.where(m, nab, 0.0).astype(BF16)
        db = d.astype(BF16)
        d = d + _bdot(_bdot(db, e).astype(BF16), db)
    t = d.astype(BF16)
    w1 = _bdot(aak, vv).astype(BF16)
    au = _bdot(t, jnp.concatenate([at, w1], axis=2))
    ah = au[:, :, 0:PAIR].astype(BF16)
    uh = au[:, :, PAIR:2 * PAIR]
    kv = _bdot_tn(kd, vv)
    pcm = jnp.swapaxes(pc_s[...], 1, 2)
    rbk = jnp.concatenate([rb, rk], axis=2)

    h = state_ref[...]
    for c in range(nsub):
        sl = slice(c * npair, (c + 1) * npair)
        arh = _bdot(jnp.concatenate([ah[sl], rt[sl]], axis=1), h.astype(BF16))
        u = (arh[:, 0:PAIR] + uh[sl]).astype(BF16)
        h = pcm[sl] * h + _bdot_tn(bd[sl], u) + kv[sl]
        y = arh[:, PAIR:2 * PAIR] + _bdot(rbk[sl], jnp.concatenate([u, vv[sl]], axis=1))
        for j in range(npair):
            y_s[SUB * c:SUB * (c + 1), LANES * j:LANES * (j + 1)] = y[j, 0:SUB] + y[j, SUB:PAIR]
    state_ref[...] = h

    y = y_s[...]
    inv_n = 1.0 / RWKV_HEAD_DIM
    mean = _head_sum(y, hsum) * inv_n
    yc = y - mean
    var = _head_sum(yc * yc, hsum) * inv_n
    y = yc * lax.rsqrt(var + RWKV_GN_EPS) * lnw_ref[...] + lnb_ref[...]
    y = y + _head_sum(r * k2 * rk_ref[...], hsum) * v
    gate = rw_ref[:, RWKV_SHIFT_COLS:RWKV_SHIFT_COLS + RWKV_WIDTH].astype(F32)
    y_ref[...] = (y * gate * jax.nn.sigmoid(gate)).astype(y_ref.dtype)


def _rwkv(o_rw, v_first, prm, consts, bsz, seq, tc):
    has_vres = v_first is not None
    n = bsz * seq
    nt = seq // tc
    row = lambda b, t: (b * nt + t, 0)
    args = [o_rw]
    specs = [pl.BlockSpec((tc, RW_COLS), row)]
    if has_vres:
        args.append(v_first)
        specs.append(pl.BlockSpec((tc, RWKV_WIDTH), row))
    names = ["mu", "w0", "w_up", "a0", "a_up", "k_k", "k_a", "r_k", "ln_w", "ln_b"]
    if has_vres:
        names += ["v0", "v_up"]
    for nm in names:
        args.append(prm[nm])
        specs.append(_const_spec(prm[nm].shape))
    for c in (consts["hsum"], consts["cum"]):
        args.append(c)
        specs.append(_const_spec(c.shape))
    out_shape = [jax.ShapeDtypeStruct((n, RWKV_WIDTH), BF16)]
    out_specs = [pl.BlockSpec((tc, RWKV_WIDTH), row)]
    if not has_vres:
        out_shape.append(jax.ShapeDtypeStruct((n, RWKV_WIDTH), F32))
        out_specs.append(pl.BlockSpec((tc, RWKV_WIDTH), row))
    scratch = [pltpu.VMEM((8, RWKV_SHIFT_COLS), F32),
               pltpu.VMEM((RWKV_HEADS // 2, PAIR, PAIR), F32)]
    nb = (tc // SUB) * (RWKV_HEADS // 2)
    scratch += [pltpu.VMEM((nb, PAIR, PAIR), BF16) for _ in range(7)]
    scratch += [pltpu.VMEM((nb, PAIR, PAIR), F32), pltpu.VMEM((tc, RWKV_WIDTH), F32)]
    res = pl.pallas_call(
        functools.partial(_rwkv_kernel, has_vres, tc),
        grid=(bsz, nt),
        in_specs=specs,
        out_specs=out_specs,
        out_shape=out_shape,
        scratch_shapes=scratch,
        compiler_params=pltpu.CompilerParams(dimension_semantics=("arbitrary", "arbitrary"),
                                             vmem_limit_bytes=VMEM_LIMIT),
        name="rwkv_vres" if has_vres else "rwkv",
    )(*args)
    return (res[0], v_first) if has_vres else (res[0], res[1])


def _gelu(x):
    return 0.5 * x * (1.0 + lax.erf(x * np.float32(1.0 / np.sqrt(2.0))))


def _gmlp_kernel(tm, g_ref, lnw_ref, lnb_ref, ws_ref, bias_ref, o_ref):
    u = _gelu(g_ref[:, 0:GMLP_WIDTH].astype(F32))
    v = _gelu(g_ref[:, GMLP_WIDTH:2 * GMLP_WIDTH].astype(F32))
    mu = jnp.mean(v, axis=-1, keepdims=True)
    vc = v - mu
    var = jnp.mean(vc * vc, axis=-1, keepdims=True)
    v = vc * lax.rsqrt(var + GMLP_LN_EPS) * lnw_ref[...] + lnb_ref[...]
    gate = g_ref[:, 2 * GMLP_WIDTH:3 * GMLP_WIDTH].astype(F32)
    ug = u * gate * jax.nn.sigmoid(gate)
    lane = lax.broadcasted_iota(jnp.int32, (GMLP_CHUNK, LANES), 1)
    first = lane < (LANES // 2)
    ri = lax.broadcasted_iota(jnp.int32, (GMLP_CHUNK, 2 * GMLP_CHUNK), 0)
    ci = lax.broadcasted_iota(jnp.int32, (GMLP_CHUNK, 2 * GMLP_CHUNK), 1)
    causal = ri >= (ci % GMLP_CHUNK)
    for j in range(GMLP_GROUPS // 2):
        w = jnp.where(causal, ws_ref[j], 0.0).astype(BF16)
        ls = slice(LANES * j, LANES * (j + 1))
        for c in range(tm // GMLP_CHUNK):
            rs = slice(GMLP_CHUNK * c, GMLP_CHUNK * (c + 1))
            vp = v[rs, ls]
            stacked = jnp.concatenate([jnp.where(first, vp, 0.0), jnp.where(first, 0.0, vp)], axis=0).astype(BF16)
            vm = _dot(w, stacked) + bias_ref[:, ls]
            o_ref[rs, ls] = (ug[rs, ls] * vm).astype(o_ref.dtype)


def _gmlp(o_g, prm, tm):
    n = o_g.shape[0]
    return pl.pallas_call(
        functools.partial(_gmlp_kernel, tm),
        grid=(n // tm,),
        in_specs=[pl.BlockSpec((tm, G_COLS), lambda i: (i, 0)),
                  _const_spec((1, GMLP_WIDTH)), _const_spec((1, GMLP_WIDTH)),
                  _const_spec(prm["w_s"].shape), _const_spec(prm["bias"].shape)],
        out_specs=pl.BlockSpec((tm, GMLP_WIDTH), lambda i: (i, 0)),
        out_shape=jax.ShapeDtypeStruct((n, GMLP_WIDTH), BF16),
        compiler_params=pltpu.CompilerParams(dimension_semantics=("arbitrary",), vmem_limit_bytes=VMEM_LIMIT),
        name="gmlp",
    )(o_g, prm["ln_w"], prm["ln_b"], prm["w_s"], prm["bias"])


def _rms(x, gain):
    return x * lax.rsqrt(jnp.mean(x * x, axis=-1, keepdims=True) + NORM_EPS) * gain


def _rope_table_kernel(pos_ref, invf_ref, o_ref):
    ang = pos_ref[...].astype(F32) * invf_ref[...]
    lane = lax.broadcasted_iota(jnp.int32, ang.shape, 1)
    o_ref[...] = jnp.where((lane % MLA_ROPE) < MLA_ROPE // 2, jnp.cos(ang), jnp.sin(ang))


def _rope_table(positions, inv_freq):
    n = positions.size
    per_row = LANES // MLA_ROPE
    pos = jnp.repeat(positions.reshape(n // per_row, per_row), MLA_ROPE, axis=1)
    invf = jnp.tile(inv_freq, LANES // inv_freq.size).reshape(1, LANES)
    rows = n // per_row
    tr = min(1024, rows)
    tab = pl.pallas_call(
        _rope_table_kernel,
        grid=(rows // tr,),
        in_specs=[pl.BlockSpec((tr, LANES), lambda i: (i, 0)), _const_spec((1, LANES))],
        out_specs=pl.BlockSpec((tr, LANES), lambda i: (i, 0)),
        out_shape=jax.ShapeDtypeStruct((rows, LANES), F32),
        name="rope_table",
    )(pos, invf)
    return tab.reshape(n, MLA_ROPE)


def _rope_spread():
    half = MLA_ROPE // 2
    m = np.zeros((MLA_ROPE, 2 * LANES), np.float32)
    for i in range(half):
        m[i, ROPE_LANE0 + i] = 1.0
        m[i, ROPE_LANE0 + half + i] = 1.0
        m[half + i, LANES + ROPE_LANE0 + i] = -1.0
        m[half + i, LANES + ROPE_LANE0 + half + i] = 1.0
    return jnp.asarray(m, BF16)


def _with_swapped_rope(w):
    half = MLA_ROPE // 2
    lo, hi = ROPE_LANE0, ROPE_LANE0 + MLA_ROPE
    return jnp.concatenate([w[..., :hi], w[..., lo + half:hi], w[..., lo:lo + half]], axis=-1)


def _mla_prep_kernel(m_ref, cs_ref, spread_ref, qn_ref, wq_ref, kvn_ref, wkv_ref, q_ref, k_ref, v_ref):
    c0 = MLA_WIDTH
    c_q = m_ref[:, c0:c0 + MLA_Q_LORA].astype(F32)
    c_kv = m_ref[:, c0 + MLA_Q_LORA:c0 + MLA_Q_LORA + MLA_KV_LORA].astype(F32)
    k_rope = m_ref[:, c0 + MLA_Q_LORA + MLA_KV_LORA:MLA_COLS].astype(F32)
    tab = _dot_exact_rhs(cs_ref[...], spread_ref[...])
    lane = lax.broadcasted_iota(jnp.int32, (1, LANES), 1)
    cos = tab[:, 0:LANES] + jnp.where(lane < ROPE_LANE0, 1.0, 0.0)
    sin = tab[:, LANES:2 * LANES]

    def rope(x):
        return x * cos + pltpu.roll(x, LANES - MLA_ROPE, axis=1) * sin

    scale = np.float32((MLA_NOPE + MLA_ROPE) ** -0.5)
    q = _dot(_rms(c_q, qn_ref[...]).astype(BF16), wq_ref[...])
    kv = _dot(_rms(c_kv, kvn_ref[...]).astype(BF16), wkv_ref[...])
    kr = rope(k_rope)
    for h in range(MLA_HEADS):
        ls = slice(HEAD_PAD * h, HEAD_PAD * (h + 1))
        q_ref[:, ls] = (rope(q[:, ls]) * scale).astype(BF16)
        k_ref[:, ls] = (kv[:, ls] + kr).astype(BF16)
    vl = lax.broadcasted_iota(jnp.int32, (1, MLA_HEADS * HEAD_PAD), 1)
    ones_lane = (vl % (2 * HEAD_PAD)) == MLA_V
    v_ref[...] = jnp.where(ones_lane | ((vl % (2 * HEAD_PAD)) == HEAD_PAD), 1.0, kv[:, MLA_HEADS * HEAD_PAD:]).astype(BF16)


def _mla_prep(o_mla, rope_tab, prm, tm):
    n = o_mla.shape[0]
    wide = MLA_HEADS * HEAD_PAD
    return pl.pallas_call(
        _mla_prep_kernel,
        grid=(n // tm,),
        in_specs=[pl.BlockSpec((tm, MLA_COLS), lambda i: (i, 0)),
                  pl.BlockSpec((tm, MLA_ROPE), lambda i: (i, 0)),
                  _const_spec(prm["spread"].shape),
                  _const_spec((1, MLA_Q_LORA)), _const_spec(prm["w_uq"].shape),
                  _const_spec((1, MLA_KV_LORA)), _const_spec(prm["w_ukv"].shape)],
        out_specs=[pl.BlockSpec((tm, wide), lambda i: (i, 0))] * 3,
        out_shape=[jax.ShapeDtypeStruct((n, wide), BF16)] * 3,
        compiler_params=pltpu.CompilerParams(dimension_semantics=("arbitrary",), vmem_limit_bytes=VMEM_LIMIT),
        name="mla_prep",
    )(o_mla, rope_tab, prm["spread"], prm["q_norm"], prm["w_uq"], prm["kv_norm"], prm["w_ukv"])


def _attn_kernel(tq, q_ref, k_ref, v_ref, gate_ref, o_ref, m_s, acc_s):
    i = pl.program_id(2)
    ri = lax.broadcasted_iota(jnp.int32, (tq, tq), 0)
    ci = lax.broadcasted_iota(jnp.int32, (tq, tq), 1)
    causal = ci <= ri
    heads = [slice(HEAD_PAD * hh, HEAD_PAD * (hh + 1)) for hh in range(2)]
    m_s[...] = jnp.full(m_s.shape, -jnp.inf, F32)
    acc_s[...] = jnp.zeros(acc_s.shape, F32)

    def block(j, masked):
        r0 = pl.multiple_of(j * tq, tq)
        for hh, ls in enumerate(heads):
            s = _dot_nt(q_ref[:, ls], k_ref[pl.ds(r0, tq), ls])
            if masked:
                s = jnp.where(causal, s, -jnp.inf)
            m_old = m_s[hh]
            m_new = jnp.maximum(m_old, jnp.max(s, axis=-1, keepdims=True))
            p = jnp.exp(s - pltpu.repeat(m_new, tq // LANES, axis=1)).astype(BF16)
            acc_s[hh] = jnp.exp(m_old - m_new) * acc_s[hh] + _dot(p, v_ref[pl.ds(r0, tq), ls])
            m_s[hh] = m_new

    def full_block(j, c):
        block(j, False)
        return c

    lax.fori_loop(0, i, full_block, 0)
    block(i, True)
    acc_a = acc_s[0]
    acc_b = acc_s[1]
    lane = lax.broadcasted_iota(jnp.int32, (tq, LANES), 1)
    out = jnp.where(lane < MLA_V, acc_a * (1.0 / acc_a[:, MLA_V:MLA_V + 1]), acc_b * (1.0 / acc_b[:, 0:1]))
    gate = gate_ref[...].astype(F32)
    o_ref[...] = (out * gate * jax.nn.sigmoid(gate)).astype(o_ref.dtype)


def _attn(q, k, vz, o_mla, bsz, seq, tq):
    n = bsz * seq
    nq = seq // tq
    pairs = MLA_HEADS // 2
    return pl.pallas_call(
        functools.partial(_attn_kernel, tq),
        grid=(bsz, pairs, nq),
        in_specs=[pl.BlockSpec((tq, 2 * HEAD_PAD), lambda b, h, i: (b * nq + i, h)),
                  pl.BlockSpec((seq, 2 * HEAD_PAD), lambda b, h, i: (b, h)),
                  pl.BlockSpec((seq, 2 * HEAD_PAD), lambda b, h, i: (b, h)),
                  pl.BlockSpec((tq, LANES), lambda b, h, i: (b * nq + i, h))],
        out_specs=pl.BlockSpec((tq, LANES), lambda b, h, i: (b * nq + i, h)),
        out_shape=jax.ShapeDtypeStruct((n, MLA_WIDTH), BF16),
        scratch_shapes=[pltpu.VMEM((2, tq, LANES), F32), pltpu.VMEM((2, tq, LANES), F32)],
        compiler_params=pltpu.CompilerParams(dimension_semantics=("arbitrary",) * 3, vmem_limit_bytes=VMEM_LIMIT),
        name="attn",
    )(q, k, vz, o_mla)


def _merge_kernel(ya_ref, yb_ref, yc_ref, mg_ref, x_ref, wb_ref, wo_ref, pn_ref, o_ref):
    ys = (ya_ref[...], yb_ref[...], yc_ref[...])
    halves = []
    for c in range(D_MODEL // 512):
        acc = None
        for nb in range(N_BRANCHES):
            cols = slice(nb * D_MODEL + c * 512, nb * D_MODEL + (c + 1) * 512)
            term = jax.nn.sigmoid(mg_ref[:, cols].astype(F32)) * _dot(ys[nb], wb_ref[nb, :, c * 512:(c + 1) * 512])
            acc = term if acc is None else acc + term
        halves.append(acc.astype(BF16))
    mixed = _dot(jnp.concatenate(halves, axis=1), wo_ref[...])
    ms = jnp.mean(mixed * mixed, axis=-1, keepdims=True)
    o_ref[...] = x_ref[...] + mixed * lax.rsqrt(ms + NORM_EPS) * pn_ref[...]


def _merge(ya, yb, yc, o_mg, x2, w_branch, w_out, post_norm, tm):
    n = x2.shape[0]
    rowspec = lambda c: pl.BlockSpec((tm, c), lambda i: (i, 0))
    return pl.pallas_call(
        _merge_kernel,
        grid=(n // tm,),
        in_specs=[rowspec(BRANCH_WIDTH), rowspec(BRANCH_WIDTH), rowspec(BRANCH_WIDTH), rowspec(MG_COLS),
                  rowspec(D_MODEL), _const_spec(w_branch.shape), _const_spec(w_out.shape), _const_spec((1, D_MODEL))],
        out_specs=rowspec(D_MODEL),
        out_shape=jax.ShapeDtypeStruct((n, D_MODEL), F32),
        compiler_params=pltpu.CompilerParams(dimension_semantics=("arbitrary",), vmem_limit_bytes=VMEM_LIMIT),
        name="merge",
    )(ya, yb, yc, o_mg, x2, w_branch, w_out, post_norm)


def _row(v):
    return v.reshape(1, -1).astype(F32)


def _pad_rows(w, rows, at):
    out = jnp.zeros((rows, w.shape[1]), w.dtype)
    return out.at[at:at + w.shape[0]].set(w)


def _layer_weights(l, w_in, rwkv_v_down, mla_w_uq, mla_w_ukv):
    o = np.cumsum((0, RWKV_SHIFT_COLS, RWKV_WIDTH, GMLP_WIDTH, GMLP_WIDTH, GMLP_WIDTH,
                   MLA_Q_LORA, MLA_KV_LORA, MLA_ROPE, MLA_WIDTH, N_BRANCHES * D_MODEL))
    w = w_in[l]
    seg = lambda i: w[:, int(o[i]):int(o[i + 1])]
    misc_a = jnp.zeros((D_MODEL, MISC), F32)
    if l > 0:
        misc_a = misc_a.at[:, 0:RWKV_VRES_LORA].set(rwkv_v_down[l - 1])
    misc_b = _with_swapped_rope(jnp.zeros((D_MODEL, MISC), F32).at[:, ROPE_LANE0:ROPE_LANE0 + MLA_ROPE].set(seg(7)))
    w_all = jnp.concatenate([seg(0), seg(1), misc_a, seg(2), seg(3), seg(4),
                             seg(8), seg(5), seg(6), misc_b, seg(9)], axis=1).astype(BF16)
    qd = MLA_NOPE + MLA_ROPE
    wq = mla_w_uq[l].reshape(MLA_Q_LORA, MLA_HEADS, qd)
    wq = _with_swapped_rope(jnp.pad(wq, ((0, 0), (0, 0), (0, HEAD_PAD - qd)))).reshape(MLA_Q_LORA, MLA_HEADS * HEAD_PAD)
    wkv = mla_w_ukv[l].reshape(MLA_KV_LORA, MLA_HEADS, MLA_NOPE + MLA_V)
    wk = jnp.pad(wkv[:, :, :MLA_NOPE], ((0, 0), (0, 0), (0, HEAD_PAD - MLA_NOPE)))
    wv = wkv[:, :, MLA_NOPE:].reshape(MLA_KV_LORA, MLA_HEADS // 2, 2, MLA_V)
    z = jnp.zeros_like(wv[:, :, 0])
    wv = jnp.stack([jnp.concatenate([wv[:, :, 0], z], axis=-1), jnp.concatenate([z, wv[:, :, 1]], axis=-1)], axis=2)
    wkv_all = jnp.concatenate([wk.reshape(MLA_KV_LORA, -1), wv.reshape(MLA_KV_LORA, -1)], axis=1)
    return w_all, wq.astype(BF16), wkv_all.astype(BF16)


def _constants(tc):
    head = np.arange(LANES) // RWKV_HEAD_DIM
    hsum = (head[:, None] == head[None, :]).astype(np.float32)
    t = np.arange(tc)
    same = (t[:, None] // SUB) == (t[None, :] // SUB)
    cum = np.concatenate([same & (t[:, None] >= t[None, :]), same], axis=0).astype(np.float32)
    return {"hsum": jnp.asarray(hsum, BF16), "cum": jnp.asarray(cum, BF16)}


def kernel(x, positions, w_in, pre_norm, post_norm, rwkv_mu, rwkv_w0, rwkv_w_up, rwkv_a0, rwkv_a_up, rwkv_k_k, rwkv_k_a, rwkv_r_k, rwkv_ln_w, rwkv_ln_b, rwkv_v0, rwkv_v_down, rwkv_v_up, gmlp_ln_w, gmlp_ln_b, gmlp_w_s, gmlp_b_s, mla_q_norm, mla_w_uq, mla_kv_norm, mla_w_ukv, w_branch, w_out):
    bsz, seq, _ = x.shape
    n = bsz * seq
    depth = w_in.shape[0]
    tc = min(256, seq)
    tm = min(256, n)
    tq = min(512, seq)
    consts = _constants(tc)
    inv_freq = 1.0 / (ROPE_THETA ** (jnp.arange(0, MLA_ROPE, 2, dtype=F32) / MLA_ROPE))
    rope_tab = _rope_table(positions, inv_freq)
    spread = _rope_spread()

    x2 = x.reshape(n, D_MODEL)
    v_first = None
    for l in range(depth):
        w_all, wq, wkv = _layer_weights(l, w_in, rwkv_v_down, mla_w_uq, mla_w_ukv)
        o_rw, o_g, o_mla, o_mg = _in_proj(x2, _row(pre_norm[l]), w_all, tm)

        rprm = {"mu": _row(rwkv_mu[l]), "w0": _row(rwkv_w0[l]), "a0": _row(rwkv_a0[l]),
                "w_up": _pad_rows(rwkv_w_up[l], LANES, 0).astype(BF16),
                "a_up": _pad_rows(rwkv_a_up[l], LANES, RWKV_DECAY_LORA).astype(BF16),
                "k_k": _row(rwkv_k_k[l]), "k_a": _row(rwkv_k_a[l]), "r_k": _row(rwkv_r_k[l]),
                "ln_w": _row(rwkv_ln_w[l]), "ln_b": _row(rwkv_ln_b[l])}
        if l > 0:
            rprm["v0"] = _row(rwkv_v0[l - 1])
            rprm["v_up"] = _pad_rows(rwkv_v_up[l - 1], MISC, 0).astype(BF16)
        y_a, v_l = _rwkv(o_rw, v_first, rprm, consts, bsz, seq, tc)
        if l == 0:
            v_first = v_l

        ws = gmlp_w_s[l].reshape(GMLP_GROUPS // 2, 2, GMLP_CHUNK, GMLP_CHUNK)
        gprm = {"ln_w": _row(gmlp_ln_w[l]), "ln_b": _row(gmlp_ln_b[l]),
                "w_s": jnp.concatenate([ws[:, 0], ws[:, 1]], axis=-1),
                "bias": jnp.repeat(gmlp_b_s[l].T, GMLP_WIDTH // GMLP_GROUPS, axis=1)}
        y_b = _gmlp(o_g, gprm, tm)

        mprm = {"spread": spread, "q_norm": _row(mla_q_norm[l]), "w_uq": wq,
                "kv_norm": _row(mla_kv_norm[l]), "w_ukv": wkv}
        q, k, vz = _mla_prep(o_mla, rope_tab, mprm, tm)
        y_c = _attn(q, k, vz, o_mla, bsz, seq, tq)

        x2 = _merge(y_a, y_b, y_c, o_mg, x2, w_branch[l].astype(BF16), w_out[l].astype(BF16),
                    _row(post_norm[l]), tm)
    return x2.reshape(bsz, seq, D_MODEL)
```

```python
import functools

import numpy as np
import jax
import jax.numpy as jnp
from jax import lax
from jax.experimental import pallas as pl
from jax.experimental.pallas import tpu as pltpu

F32 = jnp.float32
BF16 = jnp.bfloat16

D_MODEL = 1024
BRANCH_WIDTH = 512
N_BRANCHES = 3
NORM_EPS = 1e-6
RWKV_HEADS = 8
RWKV_HEAD_DIM = 64
RWKV_WIDTH = RWKV_HEADS * RWKV_HEAD_DIM
RWKV_DECAY_LORA = 64
RWKV_AAA_LORA = 64
RWKV_VRES_LORA = 32
RWKV_GN_EPS = 64e-5
RWKV_SHIFT_COLS = 3 * RWKV_WIDTH + RWKV_DECAY_LORA + RWKV_AAA_LORA
GMLP_WIDTH = 512
GMLP_GROUPS = 8
GMLP_CHUNK = 128
GMLP_LN_EPS = 1e-5
MLA_HEADS = 8
MLA_Q_LORA = 384
MLA_KV_LORA = 256
MLA_NOPE = 64
MLA_ROPE = 32
MLA_V = 64
MLA_WIDTH = MLA_HEADS * MLA_V
ROPE_THETA = 10000.0

LANES = 128
MISC = LANES
RW_COLS = RWKV_SHIFT_COLS + RWKV_WIDTH + MISC
G_COLS = 3 * GMLP_WIDTH
MLA_COLS = MLA_WIDTH + MLA_Q_LORA + MLA_KV_LORA + MISC
MG_COLS = N_BRANCHES * D_MODEL
ALL_COLS = RW_COLS + G_COLS + MLA_COLS + MG_COLS
HEAD_PAD = LANES
ROPE_LANE0 = MLA_NOPE
SUB = 64
PAIR = 2 * SUB

VMEM_LIMIT = 56 * 1024 * 1024


def _dot(a, b):
    return jnp.dot(a, b, preferred_element_type=F32)


def _dot_nt(a, b):
    return lax.dot_general(a, b, (((1,), (1,)), ((), ())), preferred_element_type=F32)


def _bdot(a, b):
    return lax.dot_general(a, b, (((2,), (1,)), ((0,), (0,))), preferred_element_type=F32)


def _bdot_nt(a, b):
    return lax.dot_general(a, b, (((2,), (2,)), ((0,), (0,))), preferred_element_type=F32)


def _bdot_tn(a, b):
    return lax.dot_general(a, b, (((1,), (1,)), ((0,), (0,))), preferred_element_type=F32)


def _split3(x):
    hi = x.astype(BF16)
    r1 = x - hi.astype(F32)
    mid = r1.astype(BF16)
    lo = (r1 - mid.astype(F32)).astype(BF16)
    return hi, mid, lo


def _dot_split2_lhs(m, x):
    hi = x.astype(BF16)
    lo = (x - hi.astype(F32)).astype(BF16)
    return _dot(m, hi) + _dot(m, lo)


def _dot_exact_rhs(x, m):
    hi, mid, lo = _split3(x)
    return _dot(hi, m) + _dot(mid, m) + _dot(lo, m)


def _head_sum(x, m):
    x16 = x.astype(BF16)
    return jnp.concatenate([_dot(x16[:, c:c + LANES], m) for c in range(0, x.shape[1], LANES)], axis=1)


def _const_spec(shape):
    nd = len(shape)
    return pl.BlockSpec(shape, lambda *_: (0,) * nd)


def _in_proj_kernel(x_ref, g_ref, w_ref, o_rw, o_g, o_mla, o_mg):
    x = x_ref[...]
    ms = jnp.mean(x * x, axis=-1, keepdims=True)
    h = (x * lax.rsqrt(ms + NORM_EPS) * g_ref[...]).astype(BF16)
    col = 0
    for o in (o_rw, o_g, o_mla, o_mg):
        width = o.shape[1]
        for c0 in range(0, width, 512):
            cw = min(512, width - c0)
            o[:, c0:c0 + cw] = _dot(h, w_ref[:, col + c0:col + c0 + cw]).astype(o.dtype)
        col += width


def _in_proj(x2, gain, w_all, tm):
    n = x2.shape[0]
    outs = (RW_COLS, G_COLS, MLA_COLS, MG_COLS)
    return pl.pallas_call(
        _in_proj_kernel,
        grid=(n // tm,),
        in_specs=[pl.BlockSpec((tm, D_MODEL), lambda i: (i, 0)),
                  _const_spec((1, D_MODEL)),
                  pl.BlockSpec((D_MODEL, ALL_COLS), lambda i: (0, 0), pipeline_mode=pl.Buffered(1))],
        out_specs=[pl.BlockSpec((tm, c), lambda i: (i, 0)) for c in outs],
        out_shape=[jax.ShapeDtypeStruct((n, c), BF16) for c in outs],
        compiler_params=pltpu.CompilerParams(dimension_semantics=("arbitrary",), vmem_limit_bytes=VMEM_LIMIT),
        name="in_proj",
    )(x2, gain, w_all)


def _rwkv_kernel(has_vres, tc, *refs):
    it = iter(refs)
    rw_ref = next(it)
    vf_ref = next(it) if has_vres else None
    mu_ref, w0_ref, wup_ref, a0_ref, aup_ref, kk_ref, ka_ref, rk_ref, lnw_ref, lnb_ref = (next(it) for _ in range(10))
    if has_vres:
        v0_ref, vup_ref = next(it), next(it)
    hsum_ref, cum_ref = next(it), next(it)
    y_ref = next(it)
    vout_ref = None if has_vres else next(it)
    carry_ref, state_ref = next(it), next(it)
    at_s, rt_s, bt_s, kt_s, v_s, bd_s, kd_s = (next(it) for _ in range(7))
    pc_s, y_s = next(it), next(it)

    @pl.when(pl.program_id(1) == 0)
    def _():
        carry_ref[...] = jnp.zeros_like(carry_ref)
        state_ref[...] = jnp.zeros_like(state_ref)

    p = rw_ref[:, 0:RWKV_SHIFT_COLS].astype(F32)
    row = lax.broadcasted_iota(jnp.int32, p.shape, 0)
    prev = jnp.where(row == 0, carry_ref[0:1, :], pltpu.roll(p, 1, axis=0))
    carry_ref[0:1, :] = p[tc - 1:tc, :]
    p = p + (prev - p) * mu_ref[...]
    r = p[:, 0:RWKV_WIDTH]
    k = p[:, RWKV_WIDTH:2 * RWKV_WIDTH]
    v = p[:, 2 * RWKV_WIDTH:3 * RWKV_WIDTH]
    lora_in = p[:, 3 * RWKV_WIDTH:RWKV_SHIFT_COLS]

    z = -(w0_ref[...] + _dot(jnp.tanh(lora_in).astype(BF16), wup_ref[...]))
    softplus = jnp.maximum(z, 0.0) + jnp.log1p(jnp.exp(-jnp.abs(z)))
    lw = -jnp.exp(-softplus - 0.5)
    a = jax.nn.sigmoid(a0_ref[...] + _dot(lora_in.astype(BF16), aup_ref[...]))
    if has_vres:
        vd = rw_ref[:, RWKV_SHIFT_COLS + RWKV_WIDTH:RW_COLS]
        vg = jax.nn.sigmoid(v0_ref[...] + _dot(vd, vup_ref[...]))
        v = v + (vf_ref[...] - v) * vg
    else:
        vout_ref[...] = v

    hsum = hsum_ref[...]
    kk = k * kk_ref[...]
    kk = kk * lax.rsqrt(jnp.maximum(_head_sum(kk * kk, hsum), 1e-24))
    k2 = k * (1.0 + (a - 1.0) * ka_ref[...])
    av = -kk
    bv = kk * a

    gs = _dot_split2_lhs(cum_ref[...], lw)
    g = gs[0:tc]
    gl = gs[tc:2 * tc]
    e_neg = jnp.exp(-g)
    e_end = jnp.exp(gl - g)
    lane = lax.broadcasted_iota(jnp.int32, (tc, RWKV_WIDTH), 1)
    first16 = jnp.where((lane % LANES) < RWKV_HEAD_DIM, 1.0, 0.0).astype(BF16)
    nsub = tc // SUB
    npair = RWKV_HEADS // 2

    def pieces(x):
        for c in range(nsub):
            for j in range(npair):
                yield c * npair + j, x[SUB * c:SUB * (c + 1), LANES * j:LANES * (j + 1)]

    def put(ref, x):
        x16 = x.astype(BF16)
        xa = x16 * first16
        xb = x16 - xa
        for b, piece in pieces(xa):
            ref[b, 0:SUB, :] = piece
        for b, piece in pieces(xb):
            ref[b, SUB:PAIR, :] = piece

    put(at_s, av * jnp.exp(g - lw))
    put(rt_s, r * jnp.exp(g))
    put(bt_s, bv * e_neg)
    put(kt_s, k2 * e_neg)
    put(v_s, v)
    put(bd_s, bv * e_end)
    put(kd_s, k2 * e_end)
    for b, piece in pieces(jnp.exp(gl)):
        pc_s[b, 0:SUB, :] = piece
        pc_s[b, SUB:PAIR, :] = piece

    ri = lax.broadcasted_iota(jnp.int32, (PAIR, PAIR), 0)
    ci = lax.broadcasted_iota(jnp.int32, (PAIR, PAIR), 1)
    strict = ri > ci
    incl = ri >= ci
    eye = jnp.where(ri == ci, 1.0, 0.0).astype(F32)
    levels = (4, 8, 16, 32, 64)
    lvl_mask = [strict & ((ri // b) == (ci // b)) & ((ri // (b // 2)) != (ci // (b // 2))) for b in levels]
    lvl2 = strict & ((ri // 2) == (ci // 2))

    at, rt, bt, kt, vv, bd, kd = (s[...] for s in (at_s, rt_s, bt_s, kt_s, v_s, bd_s, kd_s))
    s = _bdot_nt(jnp.concatenate([at, rt], axis=1), jnp.concatenate([bt, kt], axis=1))
    nab = jnp.where(strict, s[:, 0:PAIR, 0:PAIR], 0.0)
    aak = jnp.where(strict, s[:, 0:PAIR, PAIR:2 * PAIR], 0.0).astype(BF16)
    rb = jnp.where(incl, s[:, PAIR:2 * PAIR, 0:PAIR], 0.0).astype(BF16)
    rk = jnp.where(incl, s[:, PAIR:2 * PAIR, PAIR:2 * PAIR], 0.0).astype(BF16)
    d = eye + jnp.where(lvl2, nab, 0.0)
    for m in lvl_mask:
        e = jnp.where(m, nab, 0.0).astype(BF16)
        db = d.astype(BF16)
        d = d + _bdot(_bdot(db, e).astype(BF16), db)
    t = d.astype(BF16)
    w1 = _bdot(aak, vv).astype(BF16)
    au = _bdot(t, jnp.concatenate([at, w1], axis=2))
    ah = au[:, :, 0:PAIR].astype(BF16)
    uh = au[:, :, PAIR:2 * PAIR]
    kv = _bdot_tn(kd, vv)
    pcm = jnp.swapaxes(pc_s[...], 1, 2)
    rbk = jnp.concatenate([rb, rk], axis=2)

    h = state_ref[...]
    for c in range(nsub):
        sl = slice(c * npair, (c + 1) * npair)
        arh = _bdot(jnp.concatenate([ah[sl], rt[sl]], axis=1), h.astype(BF16))
        u = (arh[:, 0:PAIR] + uh[sl]).astype(BF16)
        h = pcm[sl] * h + _bdot_tn(bd[sl], u) + kv[sl]
        y = arh[:, PAIR:2 * PAIR] + _bdot(rbk[sl], jnp.concatenate([u, vv[sl]], axis=1))
        for j in range(npair):
            y_s[SUB * c:SUB * (c + 1), LANES * j:LANES * (j + 1)] = y[j, 0:SUB] + y[j, SUB:PAIR]
    state_ref[...] = h

    y = y_s[...]
    inv_n = 1.0 / RWKV_HEAD_DIM
    mean = _head_sum(y, hsum) * inv_n
    yc = y - mean
    var = _head_sum(yc * yc, hsum) * inv_n
    y = yc * lax.rsqrt(var + RWKV_GN_EPS) * lnw_ref[...] + lnb_ref[...]
    y = y + _head_sum(r * k2 * rk_ref[...], hsum) * v
    gate = rw_ref[:, RWKV_SHIFT_COLS:RWKV_SHIFT_COLS + RWKV_WIDTH].astype(F32)
    y_ref[...] = (y * gate * jax.nn.sigmoid(gate)).astype(y_ref.dtype)


def _rwkv(o_rw, v_first, prm, consts, bsz, seq, tc):
    has_vres = v_first is not None
    n = bsz * seq
    nt = seq // tc
    row = lambda b, t: (b * nt + t, 0)
    args = [o_rw]
    specs = [pl.BlockSpec((tc, RW_COLS), row)]
    if has_vres:
        args.append(v_first)
        specs.append(pl.BlockSpec((tc, RWKV_WIDTH), row))
    names = ["mu", "w0", "w_up", "a0", "a_up", "k_k", "k_a", "r_k", "ln_w", "ln_b"]
    if has_vres:
        names += ["v0", "v_up"]
    for nm in names:
        args.append(prm[nm])
        specs.append(_const_spec(prm[nm].shape))
    for c in (consts["hsum"], consts["cum"]):
        args.append(c)
        specs.append(_const_spec(c.shape))
    out_shape = [jax.ShapeDtypeStruct((n, RWKV_WIDTH), BF16)]
    out_specs = [pl.BlockSpec((tc, RWKV_WIDTH), row)]
    if not has_vres:
        out_shape.append(jax.ShapeDtypeStruct((n, RWKV_WIDTH), F32))
        out_specs.append(pl.BlockSpec((tc, RWKV_WIDTH), row))
    scratch = [pltpu.VMEM((8, RWKV_SHIFT_COLS), F32),
               pltpu.VMEM((RWKV_HEADS // 2, PAIR, PAIR), F32)]
    nb = (tc // SUB) * (RWKV_HEADS // 2)
    scratch += [pltpu.VMEM((nb, PAIR, PAIR), BF16) for _ in range(7)]
    scratch += [pltpu.VMEM((nb, PAIR, PAIR), F32), pltpu.VMEM((tc, RWKV_WIDTH), F32)]
    res = pl.pallas_call(
        functools.partial(_rwkv_kernel, has_vres, tc),
        grid=(bsz, nt),
        in_specs=specs,
        out_specs=out_specs,
        out_shape=out_shape,
        scratch_shapes=scratch,
        compiler_params=pltpu.CompilerParams(dimension_semantics=("arbitrary", "arbitrary"),
                                             vmem_limit_bytes=VMEM_LIMIT),
        name="rwkv_vres" if has_vres else "rwkv",
    )(*args)
    return (res[0], v_first) if has_vres else (res[0], res[1])


def _gelu(x):
    return 0.5 * x * (1.0 + lax.erf(x * np.float32(1.0 / np.sqrt(2.0))))


def _gmlp_kernel(tm, g_ref, lnw_ref, lnb_ref, ws_ref, bias_ref, o_ref):
    u = _gelu(g_ref[:, 0:GMLP_WIDTH].astype(F32))
    v = _gelu(g_ref[:, GMLP_WIDTH:2 * GMLP_WIDTH].astype(F32))
    mu = jnp.mean(v, axis=-1, keepdims=True)
    vc = v - mu
    var = jnp.mean(vc * vc, axis=-1, keepdims=True)
    v = vc * lax.rsqrt(var + GMLP_LN_EPS) * lnw_ref[...] + lnb_ref[...]
    gate = g_ref[:, 2 * GMLP_WIDTH:3 * GMLP_WIDTH].astype(F32)
    ug = u * gate * jax.nn.sigmoid(gate)
    lane = lax.broadcasted_iota(jnp.int32, (GMLP_CHUNK, LANES), 1)
    first = lane < (LANES // 2)
    ri = lax.broadcasted_iota(jnp.int32, (GMLP_CHUNK, 2 * GMLP_CHUNK), 0)
    ci = lax.broadcasted_iota(jnp.int32, (GMLP_CHUNK, 2 * GMLP_CHUNK), 1)
    causal = ri >= (ci % GMLP_CHUNK)
    for j in range(GMLP_GROUPS // 2):
        w = jnp.where(causal, ws_ref[j], 0.0).astype(BF16)
        ls = slice(LANES * j, LANES * (j + 1))
        for c in range(tm // GMLP_CHUNK):
            rs = slice(GMLP_CHUNK * c, GMLP_CHUNK * (c + 1))
            vp = v[rs, ls]
            stacked = jnp.concatenate([jnp.where(first, vp, 0.0), jnp.where(first, 0.0, vp)], axis=0).astype(BF16)
            vm = _dot(w, stacked) + bias_ref[:, ls]
            o_ref[rs, ls] = (ug[rs, ls] * vm).astype(o_ref.dtype)


def _gmlp(o_g, prm, tm):
    n = o_g.shape[0]
    return pl.pallas_call(
        functools.partial(_gmlp_kernel, tm),
        grid=(n // tm,),
        in_specs=[pl.BlockSpec((tm, G_COLS), lambda i: (i, 0)),
                  _const_spec((1, GMLP_WIDTH)), _const_spec((1, GMLP_WIDTH)),
                  _const_spec(prm["w_s"].shape), _const_spec(prm["bias"].shape)],
        out_specs=pl.BlockSpec((tm, GMLP_WIDTH), lambda i: (i, 0)),
        out_shape=jax.ShapeDtypeStruct((n, GMLP_WIDTH), BF16),
        compiler_params=pltpu.CompilerParams(dimension_semantics=("arbitrary",), vmem_limit_bytes=VMEM_LIMIT),
        name="gmlp",
    )(o_g, prm["ln_w"], prm["ln_b"], prm["w_s"], prm["bias"])


def _rms(x, gain):
    return x * lax.rsqrt(jnp.mean(x * x, axis=-1, keepdims=True) + NORM_EPS) * gain


def _rope_table_kernel(pos_ref, invf_ref, o_ref):
    ang = pos_ref[...].astype(F32) * invf_ref[...]
    lane = lax.broadcasted_iota(jnp.int32, ang.shape, 1)
    o_ref[...] = jnp.where((lane % MLA_ROPE) < MLA_ROPE // 2, jnp.cos(ang), jnp.sin(ang))


def _rope_table(positions, inv_freq):
    n = positions.size
    per_row = LANES // MLA_ROPE
    pos = jnp.repeat(positions.reshape(n // per_row, per_row), MLA_ROPE, axis=1)
    invf = jnp.tile(inv_freq, LANES // inv_freq.size).reshape(1, LANES)
    rows = n // per_row
    tr = min(1024, rows)
    tab = pl.pallas_call(
        _rope_table_kernel,
        grid=(rows // tr,),
        in_specs=[pl.BlockSpec((tr, LANES), lambda i: (i, 0)), _const_spec((1, LANES))],
        out_specs=pl.BlockSpec((tr, LANES), lambda i: (i, 0)),
        out_shape=jax.ShapeDtypeStruct((rows, LANES), F32),
        name="rope_table",
    )(pos, invf)
    return tab.reshape(n, MLA_ROPE)


def _rope_spread():
    half = MLA_ROPE // 2
    m = np.zeros((MLA_ROPE, 2 * LANES), np.float32)
    for i in range(half):
        m[i, ROPE_LANE0 + i] = 1.0
        m[i, ROPE_LANE0 + half + i] = 1.0
        m[half + i, LANES + ROPE_LANE0 + i] = -1.0
        m[half + i, LANES + ROPE_LANE0 + half + i] = 1.0
    return jnp.asarray(m, BF16)


def _with_swapped_rope(w):
    half = MLA_ROPE // 2
    lo, hi = ROPE_LANE0, ROPE_LANE0 + MLA_ROPE
    return jnp.concatenate([w[..., :hi], w[..., lo + half:hi], w[..., lo:lo + half]], axis=-1)


def _mla_prep_kernel(m_ref, cs_ref, spread_ref, qn_ref, wq_ref, kvn_ref, wkv_ref, q_ref, k_ref, vt_ref):
    c0 = MLA_WIDTH
    c_q = m_ref[:, c0:c0 + MLA_Q_LORA].astype(F32)
    c_kv = m_ref[:, c0 + MLA_Q_LORA:c0 + MLA_Q_LORA + MLA_KV_LORA].astype(F32)
    k_rope = m_ref[:, c0 + MLA_Q_LORA + MLA_KV_LORA:MLA_COLS].astype(F32)
    tab = _dot_exact_rhs(cs_ref[...], spread_ref[...])
    lane = lax.broadcasted_iota(jnp.int32, (1, LANES), 1)
    cos = tab[:, 0:LANES] + jnp.where(lane < ROPE_LANE0, 1.0, 0.0)
    sin = tab[:, LANES:2 * LANES]

    def rope(x):
        return x * cos + pltpu.roll(x, LANES - MLA_ROPE, axis=1) * sin

    scale = np.float32((MLA_NOPE + MLA_ROPE) ** -0.5)
    q = _dot(_rms(c_q, qn_ref[...]).astype(BF16), wq_ref[...])
    kv = _dot(_rms(c_kv, kvn_ref[...]).astype(BF16), wkv_ref[...])
    kr = rope(k_rope)
    for h in range(MLA_HEADS):
        ls = slice(HEAD_PAD * h, HEAD_PAD * (h + 1))
        q_ref[:, ls] = (rope(q[:, ls]) * scale).astype(BF16)
        k_ref[:, ls] = (kv[:, ls] + kr).astype(BF16)
    vl = lax.broadcasted_iota(jnp.int32, (1, HEAD_PAD), 1)
    for h in range(MLA_HEADS):
        vh = kv[:, (MLA_HEADS + h) * HEAD_PAD:(MLA_HEADS + h + 1) * HEAD_PAD]
        vt_ref[0, h, 0] = jnp.where(vl == MLA_V, 1.0, vh).T.astype(BF16)


def _mla_prep(o_mla, rope_tab, prm, bsz, seq, tm):
    n = o_mla.shape[0]
    nt = seq // tm
    wide = MLA_HEADS * HEAD_PAD
    return pl.pallas_call(
        _mla_prep_kernel,
        grid=(n // tm,),
        in_specs=[pl.BlockSpec((tm, MLA_COLS), lambda i: (i, 0)),
                  pl.BlockSpec((tm, MLA_ROPE), lambda i: (i, 0)),
                  _const_spec(prm["spread"].shape),
                  _const_spec((1, MLA_Q_LORA)), _const_spec(prm["w_uq"].shape),
                  _const_spec((1, MLA_KV_LORA)), _const_spec(prm["w_ukv"].shape)],
        out_specs=[pl.BlockSpec((tm, wide), lambda i: (i, 0)), pl.BlockSpec((tm, wide), lambda i: (i, 0)),
                   pl.BlockSpec((1, MLA_HEADS, 1, HEAD_PAD, tm), lambda i: (i // nt, 0, i % nt, 0, 0))],
        out_shape=[jax.ShapeDtypeStruct((n, wide), BF16), jax.ShapeDtypeStruct((n, wide), BF16),
                   jax.ShapeDtypeStruct((bsz, MLA_HEADS, nt, HEAD_PAD, tm), BF16)],
        compiler_params=pltpu.CompilerParams(dimension_semantics=("arbitrary",), vmem_limit_bytes=VMEM_LIMIT),
        name="mla_prep",
    )(o_mla, rope_tab, prm["spread"], prm["q_norm"], prm["w_uq"], prm["kv_norm"], prm["w_ukv"])


ATTN_HEADS_PER_STEP = 4
V_ROWS = 80


def _attn_kernel(tq, nh, q_ref, k_ref, vt_ref, gate_ref, o_ref, m_s, acc_s):
    i = pl.program_id(2)
    ki = lax.broadcasted_iota(jnp.int32, (tq, tq), 0)
    qi = lax.broadcasted_iota(jnp.int32, (tq, tq), 1)
    causal = ki <= qi
    heads = [slice(HEAD_PAD * hh, HEAD_PAD * (hh + 1)) for hh in range(nh)]
    m_s[...] = jnp.full(m_s.shape, -jnp.inf, F32)
    acc_s[...] = jnp.zeros(acc_s.shape, F32)
    groups = tq // 8

    def block(j, masked):
        r0 = pl.multiple_of(j * tq, tq)
        sts = [_dot_nt(k_ref[pl.ds(r0, tq), ls], q_ref[:, ls]) for ls in heads]
        for hh, st in enumerate(sts):
            if masked:
                st = jnp.where(causal, st, -jnp.inf)
            s3 = st.reshape(groups, 8, tq)
            mc = jnp.max(s3, axis=0)
            for sh in (4, 2, 1):
                mc = jnp.maximum(mc, pltpu.roll(mc, sh, axis=0))
            m_old = m_s[hh]
            m_new = jnp.maximum(m_old, mc)
            p = jnp.exp(s3 - m_new[None]).reshape(tq, tq).astype(BF16)
            alpha = jnp.exp(m_old - m_new)
            pv = _dot(vt_ref[0, hh, j, 0:V_ROWS, :], p)
            acc_s[hh] = (acc_s[hh].reshape(V_ROWS // 8, 8, tq) * alpha[None]).reshape(V_ROWS, tq) + pv
            m_s[hh] = m_new

    def full_block(j, c):
        block(j, False)
        return c

    lax.fori_loop(0, i, full_block, 0)
    block(i, True)
    outs = []
    for hh in range(nh):
        acc = acc_s[hh]
        outs.append(acc[0:MLA_V] * (1.0 / acc[MLA_V:MLA_V + 1]))
    out = jnp.concatenate(outs, axis=0).T
    gate = gate_ref[...].astype(F32)
    o_ref[...] = (out * gate * jax.nn.sigmoid(gate)).astype(o_ref.dtype)


def _attn(q, k, vt, o_mla, bsz, seq, tq, nh):
    n = bsz * seq
    nq = seq // tq
    return pl.pallas_call(
        functools.partial(_attn_kernel, tq, nh),
        grid=(bsz, MLA_HEADS // nh, nq),
        in_specs=[pl.BlockSpec((tq, nh * HEAD_PAD), lambda b, h, i: (b * nq + i, h)),
                  pl.BlockSpec((seq, nh * HEAD_PAD), lambda b, h, i: (b, h)),
                  pl.BlockSpec((1, nh, nq, HEAD_PAD, tq), lambda b, h, i: (b, h, 0, 0, 0)),
                  pl.BlockSpec((tq, nh * MLA_V), lambda b, h, i: (b * nq + i, h))],
        out_specs=pl.BlockSpec((tq, nh * MLA_V), lambda b, h, i: (b * nq + i, h)),
        out_shape=jax.ShapeDtypeStruct((n, MLA_WIDTH), BF16),
        scratch_shapes=[pltpu.VMEM((nh, 8, tq), F32), pltpu.VMEM((nh, V_ROWS, tq), F32)],
        compiler_params=pltpu.CompilerParams(dimension_semantics=("arbitrary",) * 3, vmem_limit_bytes=VMEM_LIMIT),
        name="attn",
    )(q, k, vt, o_mla)


def _merge_kernel(ya_ref, yb_ref, yc_ref, mg_ref, x_ref, wb_ref, wo_ref, pn_ref, o_ref):
    ys = (ya_ref[...], yb_ref[...], yc_ref[...])
    halves = []
    for c in range(D_MODEL // 512):
        acc = None
        for nb in range(N_BRANCHES):
            cols = slice(nb * D_MODEL + c * 512, nb * D_MODEL + (c + 1) * 512)
            term = jax.nn.sigmoid(mg_ref[:, cols].astype(F32)) * _dot(ys[nb], wb_ref[nb, :, c * 512:(c + 1) * 512])
            acc = term if acc is None else acc + term
        halves.append(acc.astype(BF16))
    mixed = _dot(jnp.concatenate(halves, axis=1), wo_ref[...])
    ms = jnp.mean(mixed * mixed, axis=-1, keepdims=True)
    o_ref[...] = x_ref[...] + mixed * lax.rsqrt(ms + NORM_EPS) * pn_ref[...]


def _merge(ya, yb, yc, o_mg, x2, w_branch, w_out, post_norm, tm):
    n = x2.shape[0]
    rowspec = lambda c: pl.BlockSpec((tm, c), lambda i: (i, 0))
    return pl.pallas_call(
        _merge_kernel,
        grid=(n // tm,),
        in_specs=[rowspec(BRANCH_WIDTH), rowspec(BRANCH_WIDTH), rowspec(BRANCH_WIDTH), rowspec(MG_COLS),
                  rowspec(D_MODEL), _const_spec(w_branch.shape), _const_spec(w_out.shape), _const_spec((1, D_MODEL))],
        out_specs=rowspec(D_MODEL),
        out_shape=jax.ShapeDtypeStruct((n, D_MODEL), F32),
        compiler_params=pltpu.CompilerParams(dimension_semantics=("arbitrary",), vmem_limit_bytes=VMEM_LIMIT),
        name="merge",
    )(ya, yb, yc, o_mg, x2, w_branch, w_out, post_norm)


def _row(v):
    return v.reshape(1, -1).astype(F32)


def _pad_rows(w, rows, at):
    out = jnp.zeros((rows, w.shape[1]), w.dtype)
    return out.at[at:at + w.shape[0]].set(w)


def _layer_weights(l, w_in, rwkv_v_down, mla_w_uq, mla_w_ukv):
    o = np.cumsum((0, RWKV_SHIFT_COLS, RWKV_WIDTH, GMLP_WIDTH, GMLP_WIDTH, GMLP_WIDTH,
                   MLA_Q_LORA, MLA_KV_LORA, MLA_ROPE, MLA_WIDTH, N_BRANCHES * D_MODEL))
    w = w_in[l]
    seg = lambda i: w[:, int(o[i]):int(o[i + 1])]
    misc_a = jnp.zeros((D_MODEL, MISC), F32)
    if l > 0:
        misc_a = misc_a.at[:, 0:RWKV_VRES_LORA].set(rwkv_v_down[l - 1])
    misc_b = _with_swapped_rope(jnp.zeros((D_MODEL, MISC), F32).at[:, ROPE_LANE0:ROPE_LANE0 + MLA_ROPE].set(seg(7)))
    w_all = jnp.concatenate([seg(0), seg(1), misc_a, seg(2), seg(3), seg(4),
                             seg(8), seg(5), seg(6), misc_b, seg(9)], axis=1).astype(BF16)
    qd = MLA_NOPE + MLA_ROPE
    wq = mla_w_uq[l].reshape(MLA_Q_LORA, MLA_HEADS, qd)
    wq = _with_swapped_rope(jnp.pad(wq, ((0, 0), (0, 0), (0, HEAD_PAD - qd)))).reshape(MLA_Q_LORA, MLA_HEADS * HEAD_PAD)
    wkv = mla_w_ukv[l].reshape(MLA_KV_LORA, MLA_HEADS, MLA_NOPE + MLA_V)
    wk = jnp.pad(wkv[:, :, :MLA_NOPE], ((0, 0), (0, 0), (0, HEAD_PAD - MLA_NOPE)))
    wv = jnp.pad(wkv[:, :, MLA_NOPE:], ((0, 0), (0, 0), (0, HEAD_PAD - MLA_V)))
    wkv_all = jnp.concatenate([wk.reshape(MLA_KV_LORA, -1), wv.reshape(MLA_KV_LORA, -1)], axis=1)
    return w_all, wq.astype(BF16), wkv_all.astype(BF16)


def _constants(tc):
    head = np.arange(LANES) // RWKV_HEAD_DIM
    hsum = (head[:, None] == head[None, :]).astype(np.float32)
    t = np.arange(tc)
    same = (t[:, None] // SUB) == (t[None, :] // SUB)
    cum = np.concatenate([same & (t[:, None] >= t[None, :]), same], axis=0).astype(np.float32)
    return {"hsum": jnp.asarray(hsum, BF16), "cum": jnp.asarray(cum, BF16)}


def kernel(x, positions, w_in, pre_norm, post_norm, rwkv_mu, rwkv_w0, rwkv_w_up, rwkv_a0, rwkv_a_up, rwkv_k_k, rwkv_k_a, rwkv_r_k, rwkv_ln_w, rwkv_ln_b, rwkv_v0, rwkv_v_down, rwkv_v_up, gmlp_ln_w, gmlp_ln_b, gmlp_w_s, gmlp_b_s, mla_q_norm, mla_w_uq, mla_kv_norm, mla_w_ukv, w_branch, w_out):
    bsz, seq, _ = x.shape
    n = bsz * seq
    depth = w_in.shape[0]
    tc = min(256, seq)
    tm = min(256, n)
    tq = min(512, seq)
    consts = _constants(tc)
    inv_freq = 1.0 / (ROPE_THETA ** (jnp.arange(0, MLA_ROPE, 2, dtype=F32) / MLA_ROPE))
    rope_tab = _rope_table(positions, inv_freq)
    spread = _rope_spread()

    x2 = x.reshape(n, D_MODEL)
    v_first = None
    for l in range(depth):
        w_all, wq, wkv = _layer_weights(l, w_in, rwkv_v_down, mla_w_uq, mla_w_ukv)
        o_rw, o_g, o_mla, o_mg = _in_proj(x2, _row(pre_norm[l]), w_all, tm)

        rprm = {"mu": _row(rwkv_mu[l]), "w0": _row(rwkv_w0[l]), "a0": _row(rwkv_a0[l]),
                "w_up": _pad_rows(rwkv_w_up[l], LANES, 0).astype(BF16),
                "a_up": _pad_rows(rwkv_a_up[l], LANES, RWKV_DECAY_LORA).astype(BF16),
                "k_k": _row(rwkv_k_k[l]), "k_a": _row(rwkv_k_a[l]), "r_k": _row(rwkv_r_k[l]),
                "ln_w": _row(rwkv_ln_w[l]), "ln_b": _row(rwkv_ln_b[l])}
        if l > 0:
            rprm["v0"] = _row(rwkv_v0[l - 1])
            rprm["v_up"] = _pad_rows(rwkv_v_up[l - 1], MISC, 0).astype(BF16)
        y_a, v_l = _rwkv(o_rw, v_first, rprm, consts, bsz, seq, tc)
        if l == 0:
            v_first = v_l

        ws = gmlp_w_s[l].reshape(GMLP_GROUPS // 2, 2, GMLP_CHUNK, GMLP_CHUNK)
        gprm = {"ln_w": _row(gmlp_ln_w[l]), "ln_b": _row(gmlp_ln_b[l]),
                "w_s": jnp.concatenate([ws[:, 0], ws[:, 1]], axis=-1),
                "bias": jnp.repeat(gmlp_b_s[l].T, GMLP_WIDTH // GMLP_GROUPS, axis=1)}
        y_b = _gmlp(o_g, gprm, tm)

        mprm = {"spread": spread, "q_norm": _row(mla_q_norm[l]), "w_uq": wq,
                "kv_norm": _row(mla_kv_norm[l]), "w_ukv": wkv}
        q, k, vt = _mla_prep(o_mla, rope_tab, mprm, bsz, seq, tq)
        y_c = _attn(q, k, vt, o_mla, bsz, seq, tq, ATTN_HEADS_PER_STEP)

        x2 = _merge(y_a, y_b, y_c, o_mg, x2, w_branch[l].astype(BF16), w_out[l].astype(BF16),
                    _row(post_norm[l]), tm)
    return x2.reshape(bsz, seq, D_MODEL)
```

```python
import functools

import numpy as np
import jax
import jax.numpy as jnp
from jax import lax
from jax.experimental import pallas as pl
from jax.experimental.pallas import tpu as pltpu

F32 = jnp.float32
BF16 = jnp.bfloat16

D_MODEL = 1024
BRANCH_WIDTH = 512
N_BRANCHES = 3
NORM_EPS = 1e-6
RWKV_HEADS = 8
RWKV_HEAD_DIM = 64
RWKV_WIDTH = RWKV_HEADS * RWKV_HEAD_DIM
RWKV_DECAY_LORA = 64
RWKV_AAA_LORA = 64
RWKV_VRES_LORA = 32
RWKV_GN_EPS = 64e-5
RWKV_SHIFT_COLS = 3 * RWKV_WIDTH + RWKV_DECAY_LORA + RWKV_AAA_LORA
GMLP_WIDTH = 512
GMLP_GROUPS = 8
GMLP_CHUNK = 128
GMLP_LN_EPS = 1e-5
MLA_HEADS = 8
MLA_Q_LORA = 384
MLA_KV_LORA = 256
MLA_NOPE = 64
MLA_ROPE = 32
MLA_V = 64
MLA_WIDTH = MLA_HEADS * MLA_V
ROPE_THETA = 10000.0

LANES = 128
MISC = LANES
RW_COLS = RWKV_SHIFT_COLS + RWKV_WIDTH + MISC
G_COLS = 3 * GMLP_WIDTH
MLA_COLS = MLA_WIDTH + MLA_Q_LORA + MLA_KV_LORA + MISC
MG_COLS = N_BRANCHES * D_MODEL
ALL_COLS = RW_COLS + G_COLS + MLA_COLS + MG_COLS
HEAD_PAD = LANES
ROPE_LANE0 = MLA_NOPE
SUB = 64
PAIR = 2 * SUB

VMEM_LIMIT = 56 * 1024 * 1024


def _dot(a, b):
    return jnp.dot(a, b, preferred_element_type=F32)


def _dot_nt(a, b):
    return lax.dot_general(a, b, (((1,), (1,)), ((), ())), preferred_element_type=F32)


def _bdot(a, b):
    return lax.dot_general(a, b, (((2,), (1,)), ((0,), (0,))), preferred_element_type=F32)


def _bdot_nt(a, b):
    return lax.dot_general(a, b, (((2,), (2,)), ((0,), (0,))), preferred_element_type=F32)


def _bdot_tn(a, b):
    return lax.dot_general(a, b, (((1,), (1,)), ((0,), (0,))), preferred_element_type=F32)


def _split3(x):
    hi = x.astype(BF16)
    r1 = x - hi.astype(F32)
    mid = r1.astype(BF16)
    lo = (r1 - mid.astype(F32)).astype(BF16)
    return hi, mid, lo


def _dot_split2_lhs(m, x):
    hi = x.astype(BF16)
    lo = (x - hi.astype(F32)).astype(BF16)
    return _dot(m, hi) + _dot(m, lo)


def _dot_exact_rhs(x, m):
    hi, mid, lo = _split3(x)
    return _dot(hi, m) + _dot(mid, m) + _dot(lo, m)


def _head_sum(x, m):
    x16 = x.astype(BF16)
    return jnp.concatenate([_dot(x16[:, c:c + LANES], m) for c in range(0, x.shape[1], LANES)], axis=1)


def _const_spec(shape):
    nd = len(shape)
    return pl.BlockSpec(shape, lambda *_: (0,) * nd)


PROJ_CHUNK = 512


def _in_proj_kernel(tm, x_ref, g_ref, w_ref, lnw_ref, lnb_ref, ws_ref, bias_ref, o_rw, o_yb, o_mla, o_mg):
    x = x_ref[...]
    ms = jnp.mean(x * x, axis=-1, keepdims=True)
    h = (x * lax.rsqrt(ms + NORM_EPS) * g_ref[...]).astype(BF16)

    def project(o, col):
        width = o.shape[1]
        for c0 in range(0, width, PROJ_CHUNK):
            cw = min(PROJ_CHUNK, width - c0)
            o[:, c0:c0 + cw] = _dot(h, w_ref[:, col + c0:col + c0 + cw]).astype(o.dtype)

    project(o_rw, 0)
    u, v, gate = (_dot(h, w_ref[:, RW_COLS + i * GMLP_WIDTH:RW_COLS + (i + 1) * GMLP_WIDTH]) for i in range(3))
    _gmlp_mix(tm, u, v, gate, lnw_ref, lnb_ref, ws_ref, bias_ref, o_yb)
    project(o_mla, RW_COLS + G_COLS)
    project(o_mg, RW_COLS + G_COLS + MLA_COLS)


def _in_proj(x2, gain, w_all, gprm, tm):
    n = x2.shape[0]
    outs = (RW_COLS, GMLP_WIDTH, MLA_COLS, MG_COLS)
    return pl.pallas_call(
        functools.partial(_in_proj_kernel, tm),
        grid=(n // tm,),
        in_specs=[pl.BlockSpec((tm, D_MODEL), lambda i: (i, 0)),
                  _const_spec((1, D_MODEL)),
                  pl.BlockSpec((D_MODEL, ALL_COLS), lambda i: (0, 0), pipeline_mode=pl.Buffered(1)),
                  _const_spec((1, GMLP_WIDTH)), _const_spec((1, GMLP_WIDTH)),
                  _const_spec(gprm["w_s"].shape), _const_spec(gprm["bias"].shape)],
        out_specs=[pl.BlockSpec((tm, c), lambda i: (i, 0)) for c in outs],
        out_shape=[jax.ShapeDtypeStruct((n, c), BF16) for c in outs],
        compiler_params=pltpu.CompilerParams(dimension_semantics=("arbitrary",), vmem_limit_bytes=VMEM_LIMIT),
        name="in_proj",
    )(x2, gain, w_all, gprm["ln_w"], gprm["ln_b"], gprm["w_s"], gprm["bias"])


def _rwkv_kernel(has_vres, tc, *refs):
    it = iter(refs)
    rw_ref = next(it)
    vf_ref = next(it) if has_vres else None
    mu_ref, w0_ref, wup_ref, a0_ref, aup_ref, kk_ref, ka_ref, rk_ref, lnw_ref, lnb_ref = (next(it) for _ in range(10))
    if has_vres:
        v0_ref, vup_ref = next(it), next(it)
    hsum_ref, cum_ref = next(it), next(it)
    y_ref = next(it)
    vout_ref = None if has_vres else next(it)
    carry_ref, state_ref = next(it), next(it)
    at_s, rt_s, bt_s, kt_s, v_s, bd_s, kd_s = (next(it) for _ in range(7))
    pc_s, y_s = next(it), next(it)

    @pl.when(pl.program_id(1) == 0)
    def _():
        carry_ref[...] = jnp.zeros_like(carry_ref)
        state_ref[...] = jnp.zeros_like(state_ref)

    p = rw_ref[:, 0:RWKV_SHIFT_COLS].astype(F32)
    row = lax.broadcasted_iota(jnp.int32, p.shape, 0)
    prev = jnp.where(row == 0, carry_ref[0:1, :], pltpu.roll(p, 1, axis=0))
    carry_ref[0:1, :] = p[tc - 1:tc, :]
    p = p + (prev - p) * mu_ref[...]
    r = p[:, 0:RWKV_WIDTH]
    k = p[:, RWKV_WIDTH:2 * RWKV_WIDTH]
    v = p[:, 2 * RWKV_WIDTH:3 * RWKV_WIDTH]
    lora_in = p[:, 3 * RWKV_WIDTH:RWKV_SHIFT_COLS]

    z = -(w0_ref[...] + _dot(jnp.tanh(lora_in).astype(BF16), wup_ref[...]))
    softplus = jnp.maximum(z, 0.0) + jnp.log1p(jnp.exp(-jnp.abs(z)))
    lw = -jnp.exp(-softplus - 0.5)
    a = jax.nn.sigmoid(a0_ref[...] + _dot(lora_in.astype(BF16), aup_ref[...]))
    if has_vres:
        vd = rw_ref[:, RWKV_SHIFT_COLS + RWKV_WIDTH:RW_COLS]
        vg = jax.nn.sigmoid(v0_ref[...] + _dot(vd, vup_ref[...]))
        v = v + (vf_ref[...] - v) * vg
    else:
        vout_ref[...] = v

    hsum = hsum_ref[...]
    kk = k * kk_ref[...]
    kk = kk * lax.rsqrt(jnp.maximum(_head_sum(kk * kk, hsum), 1e-24))
    k2 = k * (1.0 + (a - 1.0) * ka_ref[...])
    av = -kk
    bv = kk * a

    gs = _dot_split2_lhs(cum_ref[...], lw)
    g = gs[0:tc]
    gl = gs[tc:2 * tc]
    e_neg = jnp.exp(-g)
    e_end = jnp.exp(gl - g)
    lane = lax.broadcasted_iota(jnp.int32, (tc, RWKV_WIDTH), 1)
    first16 = jnp.where((lane % LANES) < RWKV_HEAD_DIM, 1.0, 0.0).astype(BF16)
    nsub = tc // SUB
    npair = RWKV_HEADS // 2

    def pieces(x):
        for c in range(nsub):
            for j in range(npair):
                yield c * npair + j, x[SUB * c:SUB * (c + 1), LANES * j:LANES * (j + 1)]

    def put(ref, x):
        x16 = x.astype(BF16)
        xa = x16 * first16
        xb = x16 - xa
        for b, piece in pieces(xa):
            ref[b, 0:SUB, :] = piece
        for b, piece in pieces(xb):
            ref[b, SUB:PAIR, :] = piece

    put(at_s, av * jnp.exp(g - lw))
    put(rt_s, r * jnp.exp(g))
    put(bt_s, bv * e_neg)
    put(kt_s, k2 * e_neg)
    put(v_s, v)
    put(bd_s, bv * e_end)
    put(kd_s, k2 * e_end)
    for b, piece in pieces(jnp.exp(gl)):
        pc_s[b, 0:SUB, :] = piece
        pc_s[b, SUB:PAIR, :] = piece

    ri = lax.broadcasted_iota(jnp.int32, (PAIR, PAIR), 0)
    ci = lax.broadcasted_iota(jnp.int32, (PAIR, PAIR), 1)
    strict = ri > ci
    incl = ri >= ci
    eye = jnp.where(ri == ci, 1.0, 0.0).astype(F32)
    levels = (4, 8, 16, 32, 64)
    lvl_mask = [strict & ((ri // b) == (ci // b)) & ((ri // (b // 2)) != (ci // (b // 2))) for b in levels]
    lvl2 = strict & ((ri // 2) == (ci // 2))

    at, rt, bt, kt, vv, bd, kd = (s[...] for s in (at_s, rt_s, bt_s, kt_s, v_s, bd_s, kd_s))
    s = _bdot_nt(jnp.concatenate([at, rt], axis=1), jnp.concatenate([bt, kt], axis=1))
    nab = jnp.where(strict, s[:, 0:PAIR, 0:PAIR], 0.0)
    aak = jnp.where(strict, s[:, 0:PAIR, PAIR:2 * PAIR], 0.0).astype(BF16)
    rb = jnp.where(incl, s[:, PAIR:2 * PAIR, 0:PAIR], 0.0).astype(BF16)
    rk = jnp.where(incl, s[:, PAIR:2 * PAIR, PAIR:2 * PAIR], 0.0).astype(BF16)
    d = eye + jnp.where(lvl2, nab, 0.0)
    for m in lvl_mask:
        e = jnp.where(m, nab, 0.0).astype(BF16)
        db = d.astype(BF16)
        d = d + _bdot(_bdot(db, e).astype(BF16), db)
    t = d.astype(BF16)
    w1 = _bdot(aak, vv).astype(BF16)
    au = _bdot(t, jnp.concatenate([at, w1], axis=2))
    ah = au[:, :, 0:PAIR].astype(BF16)
    uh = au[:, :, PAIR:2 * PAIR]
    kv = _bdot_tn(kd, vv)
    pcm = jnp.swapaxes(pc_s[...], 1, 2)
    rbk = jnp.concatenate([rb, rk], axis=2)

    h = state_ref[...]
    for c in range(nsub):
        sl = slice(c * npair, (c + 1) * npair)
        arh = _bdot(jnp.concatenate([ah[sl], rt[sl]], axis=1), h.astype(BF16))
        u = (arh[:, 0:PAIR] + uh[sl]).astype(BF16)
        h = pcm[sl] * h + _bdot_tn(bd[sl], u) + kv[sl]
        y = arh[:, PAIR:2 * PAIR] + _bdot(rbk[sl], jnp.concatenate([u, vv[sl]], axis=1))
        for j in range(npair):
            y_s[SUB * c:SUB * (c + 1), LANES * j:LANES * (j + 1)] = y[j, 0:SUB] + y[j, SUB:PAIR]
    state_ref[...] = h

    y = y_s[...]
    inv_n = 1.0 / RWKV_HEAD_DIM
    mean = _head_sum(y, hsum) * inv_n
    yc = y - mean
    var = _head_sum(yc * yc, hsum) * inv_n
    y = yc * lax.rsqrt(var + RWKV_GN_EPS) * lnw_ref[...] + lnb_ref[...]
    y = y + _head_sum(r * k2 * rk_ref[...], hsum) * v
    gate = rw_ref[:, RWKV_SHIFT_COLS:RWKV_SHIFT_COLS + RWKV_WIDTH].astype(F32)
    y_ref[...] = (y * gate * jax.nn.sigmoid(gate)).astype(y_ref.dtype)


def _rwkv(o_rw, v_first, prm, consts, bsz, seq, tc):
    has_vres = v_first is not None
    n = bsz * seq
    nt = seq // tc
    row = lambda b, t: (b * nt + t, 0)
    args = [o_rw]
    specs = [pl.BlockSpec((tc, RW_COLS), row)]
    if has_vres:
        args.append(v_first)
        specs.append(pl.BlockSpec((tc, RWKV_WIDTH), row))
    names = ["mu", "w0", "w_up", "a0", "a_up", "k_k", "k_a", "r_k", "ln_w", "ln_b"]
    if has_vres:
        names += ["v0", "v_up"]
    for nm in names:
        args.append(prm[nm])
        specs.append(_const_spec(prm[nm].shape))
    for c in (consts["hsum"], consts["cum"]):
        args.append(c)
        specs.append(_const_spec(c.shape))
    out_shape = [jax.ShapeDtypeStruct((n, RWKV_WIDTH), BF16)]
    out_specs = [pl.BlockSpec((tc, RWKV_WIDTH), row)]
    if not has_vres:
        out_shape.append(jax.ShapeDtypeStruct((n, RWKV_WIDTH), F32))
        out_specs.append(pl.BlockSpec((tc, RWKV_WIDTH), row))
    scratch = [pltpu.VMEM((8, RWKV_SHIFT_COLS), F32),
               pltpu.VMEM((RWKV_HEADS // 2, PAIR, PAIR), F32)]
    nb = (tc // SUB) * (RWKV_HEADS // 2)
    scratch += [pltpu.VMEM((nb, PAIR, PAIR), BF16) for _ in range(7)]
    scratch += [pltpu.VMEM((nb, PAIR, PAIR), F32), pltpu.VMEM((tc, RWKV_WIDTH), F32)]
    res = pl.pallas_call(
        functools.partial(_rwkv_kernel, has_vres, tc),
        grid=(bsz, nt),
        in_specs=specs,
        out_specs=out_specs,
        out_shape=out_shape,
        scratch_shapes=scratch,
        compiler_params=pltpu.CompilerParams(dimension_semantics=("arbitrary", "arbitrary"),
                                             vmem_limit_bytes=VMEM_LIMIT),
        name="rwkv_vres" if has_vres else "rwkv",
    )(*args)
    return (res[0], v_first) if has_vres else (res[0], res[1])


def _gelu(x):
    return 0.5 * x * (1.0 + lax.erf(x * np.float32(1.0 / np.sqrt(2.0))))


def _gmlp_mix(tm, u, v, gate, lnw_ref, lnb_ref, ws_ref, bias_ref, o_ref):
    u = _gelu(u)
    v = _gelu(v)
    mu = jnp.mean(v, axis=-1, keepdims=True)
    vc = v - mu
    var = jnp.mean(vc * vc, axis=-1, keepdims=True)
    v = vc * lax.rsqrt(var + GMLP_LN_EPS) * lnw_ref[...] + lnb_ref[...]
    ug = u * gate * jax.nn.sigmoid(gate)
    lane = lax.broadcasted_iota(jnp.int32, (GMLP_CHUNK, LANES), 1)
    first = lane < (LANES // 2)
    ri = lax.broadcasted_iota(jnp.int32, (GMLP_CHUNK, 2 * GMLP_CHUNK), 0)
    ci = lax.broadcasted_iota(jnp.int32, (GMLP_CHUNK, 2 * GMLP_CHUNK), 1)
    causal = ri >= (ci % GMLP_CHUNK)
    for j in range(GMLP_GROUPS // 2):
        w = jnp.where(causal, ws_ref[j], 0.0).astype(BF16)
        ls = slice(LANES * j, LANES * (j + 1))
        for c in range(tm // GMLP_CHUNK):
            rs = slice(GMLP_CHUNK * c, GMLP_CHUNK * (c + 1))
            vp = v[rs, ls]
            stacked = jnp.concatenate([jnp.where(first, vp, 0.0), jnp.where(first, 0.0, vp)], axis=0).astype(BF16)
            vm = _dot(w, stacked) + bias_ref[:, ls]
            o_ref[rs, ls] = (ug[rs, ls] * vm).astype(o_ref.dtype)


def _rms(x, gain):
    return x * lax.rsqrt(jnp.mean(x * x, axis=-1, keepdims=True) + NORM_EPS) * gain


def _rope_table_kernel(pos_ref, invf_ref, o_ref):
    ang = pos_ref[...].astype(F32) * invf_ref[...]
    lane = lax.broadcasted_iota(jnp.int32, ang.shape, 1)
    o_ref[...] = jnp.where((lane % MLA_ROPE) < MLA_ROPE // 2, jnp.cos(ang), jnp.sin(ang))


def _rope_table(positions, inv_freq):
    n = positions.size
    per_row = LANES // MLA_ROPE
    pos = jnp.repeat(positions.reshape(n // per_row, per_row), MLA_ROPE, axis=1)
    invf = jnp.tile(inv_freq, LANES // inv_freq.size).reshape(1, LANES)
    rows = n // per_row
    tr = min(1024, rows)
    tab = pl.pallas_call(
        _rope_table_kernel,
        grid=(rows // tr,),
        in_specs=[pl.BlockSpec((tr, LANES), lambda i: (i, 0)), _const_spec((1, LANES))],
        out_specs=pl.BlockSpec((tr, LANES), lambda i: (i, 0)),
        out_shape=jax.ShapeDtypeStruct((rows, LANES), F32),
        name="rope_table",
    )(pos, invf)
    return tab.reshape(n, MLA_ROPE)


def _rope_spread():
    half = MLA_ROPE // 2
    m = np.zeros((MLA_ROPE, 2 * LANES), np.float32)
    for i in range(half):
        m[i, ROPE_LANE0 + i] = 1.0
        m[i, ROPE_LANE0 + half + i] = 1.0
        m[half + i, LANES + ROPE_LANE0 + i] = -1.0
        m[half + i, LANES + ROPE_LANE0 + half + i] = 1.0
    return jnp.asarray(m, BF16)


def _with_swapped_rope(w):
    half = MLA_ROPE // 2
    lo, hi = ROPE_LANE0, ROPE_LANE0 + MLA_ROPE
    return jnp.concatenate([w[..., :hi], w[..., lo + half:hi], w[..., lo:lo + half]], axis=-1)


def _mla_prep_kernel(m_ref, cs_ref, spread_ref, qn_ref, wq_ref, kvn_ref, wkv_ref, q_ref, k_ref, vt_ref):
    c0 = MLA_WIDTH
    c_q = m_ref[:, c0:c0 + MLA_Q_LORA].astype(F32)
    c_kv = m_ref[:, c0 + MLA_Q_LORA:c0 + MLA_Q_LORA + MLA_KV_LORA].astype(F32)
    k_rope = m_ref[:, c0 + MLA_Q_LORA + MLA_KV_LORA:MLA_COLS].astype(F32)
    tab = _dot_exact_rhs(cs_ref[...], spread_ref[...])
    lane = lax.broadcasted_iota(jnp.int32, (1, LANES), 1)
    cos = tab[:, 0:LANES] + jnp.where(lane < ROPE_LANE0, 1.0, 0.0)
    sin = tab[:, LANES:2 * LANES]

    def rope(x):
        return x * cos + pltpu.roll(x, LANES - MLA_ROPE, axis=1) * sin

    scale = np.float32((MLA_NOPE + MLA_ROPE) ** -0.5)
    q = _dot(_rms(c_q, qn_ref[...]).astype(BF16), wq_ref[...])
    kv = _dot(_rms(c_kv, kvn_ref[...]).astype(BF16), wkv_ref[...])
    kr = rope(k_rope)
    for h in range(MLA_HEADS):
        ls = slice(HEAD_PAD * h, HEAD_PAD * (h + 1))
        q_ref[:, ls] = (rope(q[:, ls]) * scale).astype(BF16)
        k_ref[:, ls] = (kv[:, ls] + kr).astype(BF16)
    vl = lax.broadcasted_iota(jnp.int32, (1, HEAD_PAD), 1)
    for h in range(MLA_HEADS):
        vh = kv[:, (MLA_HEADS + h) * HEAD_PAD:(MLA_HEADS + h + 1) * HEAD_PAD]
        vt_ref[0, h, 0] = jnp.where(vl == MLA_V, 1.0, vh).T.astype(BF16)


def _mla_prep(o_mla, rope_tab, prm, bsz, seq, tm):
    n = o_mla.shape[0]
    nt = seq // tm
    wide = MLA_HEADS * HEAD_PAD
    return pl.pallas_call(
        _mla_prep_kernel,
        grid=(n // tm,),
        in_specs=[pl.BlockSpec((tm, MLA_COLS), lambda i: (i, 0)),
                  pl.BlockSpec((tm, MLA_ROPE), lambda i: (i, 0)),
                  _const_spec(prm["spread"].shape),
                  _const_spec((1, MLA_Q_LORA)), _const_spec(prm["w_uq"].shape),
                  _const_spec((1, MLA_KV_LORA)), _const_spec(prm["w_ukv"].shape)],
        out_specs=[pl.BlockSpec((tm, wide), lambda i: (i, 0)), pl.BlockSpec((tm, wide), lambda i: (i, 0)),
                   pl.BlockSpec((1, MLA_HEADS, 1, HEAD_PAD, tm), lambda i: (i // nt, 0, i % nt, 0, 0))],
        out_shape=[jax.ShapeDtypeStruct((n, wide), BF16), jax.ShapeDtypeStruct((n, wide), BF16),
                   jax.ShapeDtypeStruct((bsz, MLA_HEADS, nt, HEAD_PAD, tm), BF16)],
        compiler_params=pltpu.CompilerParams(dimension_semantics=("arbitrary",), vmem_limit_bytes=VMEM_LIMIT),
        name="mla_prep",
    )(o_mla, rope_tab, prm["spread"], prm["q_norm"], prm["w_uq"], prm["kv_norm"], prm["w_ukv"])


ATTN_HEADS_PER_STEP = 4
V_ROWS = 80


def _attn_kernel(tq, nh, q_ref, k_ref, vt_ref, gate_ref, o_ref, m_s, acc_s):
    i = pl.program_id(2)
    ki = lax.broadcasted_iota(jnp.int32, (tq, tq), 0)
    qi = lax.broadcasted_iota(jnp.int32, (tq, tq), 1)
    causal = ki <= qi
    heads = [slice(HEAD_PAD * hh, HEAD_PAD * (hh + 1)) for hh in range(nh)]
    m_s[...] = jnp.full(m_s.shape, -jnp.inf, F32)
    acc_s[...] = jnp.zeros(acc_s.shape, F32)
    groups = tq // 8

    def block(j, masked):
        r0 = pl.multiple_of(j * tq, tq)
        sts = [_dot_nt(k_ref[pl.ds(r0, tq), ls], q_ref[:, ls]) for ls in heads]
        for hh, st in enumerate(sts):
            if masked:
                st = jnp.where(causal, st, -jnp.inf)
            s3 = st.reshape(groups, 8, tq)
            mc = jnp.max(s3, axis=0)
            for sh in (4, 2, 1):
                mc = jnp.maximum(mc, pltpu.roll(mc, sh, axis=0))
            m_old = m_s[hh]
            m_new = jnp.maximum(m_old, mc)
            p = jnp.exp(s3 - m_new[None]).reshape(tq, tq).astype(BF16)
            alpha = jnp.exp(m_old - m_new)
            pv = _dot(vt_ref[0, hh, j, 0:V_ROWS, :], p)
            acc_s[hh] = (acc_s[hh].reshape(V_ROWS // 8, 8, tq) * alpha[None]).reshape(V_ROWS, tq) + pv
            m_s[hh] = m_new

    def full_block(j, c):
        block(j, False)
        return c

    lax.fori_loop(0, i, full_block, 0)
    block(i, True)
    outs = []
    for hh in range(nh):
        acc = acc_s[hh]
        outs.append(acc[0:MLA_V] * (1.0 / acc[MLA_V:MLA_V + 1]))
    out = jnp.concatenate(outs, axis=0).T
    gate = gate_ref[...].astype(F32)
    o_ref[...] = (out * gate * jax.nn.sigmoid(gate)).astype(o_ref.dtype)


def _attn(q, k, vt, o_mla, bsz, seq, tq, nh):
    n = bsz * seq
    nq = seq // tq
    return pl.pallas_call(
        functools.partial(_attn_kernel, tq, nh),
        grid=(bsz, MLA_HEADS // nh, nq),
        in_specs=[pl.BlockSpec((tq, nh * HEAD_PAD), lambda b, h, i: (b * nq + i, h)),
                  pl.BlockSpec((seq, nh * HEAD_PAD), lambda b, h, i: (b, h)),
                  pl.BlockSpec((1, nh, nq, HEAD_PAD, tq), lambda b, h, i: (b, h, 0, 0, 0)),
                  pl.BlockSpec((tq, nh * MLA_V), lambda b, h, i: (b * nq + i, h))],
        out_specs=pl.BlockSpec((tq, nh * MLA_V), lambda b, h, i: (b * nq + i, h)),
        out_shape=jax.ShapeDtypeStruct((n, MLA_WIDTH), BF16),
        scratch_shapes=[pltpu.VMEM((nh, 8, tq), F32), pltpu.VMEM((nh, V_ROWS, tq), F32)],
        compiler_params=pltpu.CompilerParams(dimension_semantics=("arbitrary",) * 3, vmem_limit_bytes=VMEM_LIMIT),
        name="attn",
    )(q, k, vt, o_mla)


def _merge_kernel(ya_ref, yb_ref, yc_ref, mg_ref, x_ref, wb_ref, wo_ref, pn_ref, o_ref):
    ys = (ya_ref[...], yb_ref[...], yc_ref[...])
    halves = []
    for c in range(D_MODEL // 512):
        acc = None
        for nb in range(N_BRANCHES):
            cols = slice(nb * D_MODEL + c * 512, nb * D_MODEL + (c + 1) * 512)
            term = jax.nn.sigmoid(mg_ref[:, cols].astype(F32)) * _dot(ys[nb], wb_ref[nb, :, c * 512:(c + 1) * 512])
            acc = term if acc is None else acc + term
        halves.append(acc.astype(BF16))
    mixed = _dot(jnp.concatenate(halves, axis=1), wo_ref[...])
    ms = jnp.mean(mixed * mixed, axis=-1, keepdims=True)
    o_ref[...] = x_ref[...] + mixed * lax.rsqrt(ms + NORM_EPS) * pn_ref[...]


def _merge(ya, yb, yc, o_mg, x2, w_branch, w_out, post_norm, tm):
    n = x2.shape[0]
    rowspec = lambda c: pl.BlockSpec((tm, c), lambda i: (i, 0))
    return pl.pallas_call(
        _merge_kernel,
        grid=(n // tm,),
        in_specs=[rowspec(BRANCH_WIDTH), rowspec(BRANCH_WIDTH), rowspec(BRANCH_WIDTH), rowspec(MG_COLS),
                  rowspec(D_MODEL), _const_spec(w_branch.shape), _const_spec(w_out.shape), _const_spec((1, D_MODEL))],
        out_specs=rowspec(D_MODEL),
        out_shape=jax.ShapeDtypeStruct((n, D_MODEL), F32),
        compiler_params=pltpu.CompilerParams(dimension_semantics=("arbitrary",), vmem_limit_bytes=VMEM_LIMIT),
        name="merge",
    )(ya, yb, yc, o_mg, x2, w_branch, w_out, post_norm)


def _row(v):
    return v.reshape(1, -1).astype(F32)


def _pad_rows(w, rows, at):
    out = jnp.zeros((rows, w.shape[1]), w.dtype)
    return out.at[at:at + w.shape[0]].set(w)


def _layer_weights(l, w_in, rwkv_v_down, mla_w_uq, mla_w_ukv):
    o = np.cumsum((0, RWKV_SHIFT_COLS, RWKV_WIDTH, GMLP_WIDTH, GMLP_WIDTH, GMLP_WIDTH,
                   MLA_Q_LORA, MLA_KV_LORA, MLA_ROPE, MLA_WIDTH, N_BRANCHES * D_MODEL))
    w = w_in[l]
    seg = lambda i: w[:, int(o[i]):int(o[i + 1])]
    misc_a = jnp.zeros((D_MODEL, MISC), F32)
    if l > 0:
        misc_a = misc_a.at[:, 0:RWKV_VRES_LORA].set(rwkv_v_down[l - 1])
    misc_b = _with_swapped_rope(jnp.zeros((D_MODEL, MISC), F32).at[:, ROPE_LANE0:ROPE_LANE0 + MLA_ROPE].set(seg(7)))
    w_all = jnp.concatenate([seg(0), seg(1), misc_a, seg(2), seg(3), seg(4),
                             seg(8), seg(5), seg(6), misc_b, seg(9)], axis=1).astype(BF16)
    qd = MLA_NOPE + MLA_ROPE
    wq = mla_w_uq[l].reshape(MLA_Q_LORA, MLA_HEADS, qd)
    wq = _with_swapped_rope(jnp.pad(wq, ((0, 0), (0, 0), (0, HEAD_PAD - qd)))).reshape(MLA_Q_LORA, MLA_HEADS * HEAD_PAD)
    wkv = mla_w_ukv[l].reshape(MLA_KV_LORA, MLA_HEADS, MLA_NOPE + MLA_V)
    wk = jnp.pad(wkv[:, :, :MLA_NOPE], ((0, 0), (0, 0), (0, HEAD_PAD - MLA_NOPE)))
    wv = jnp.pad(wkv[:, :, MLA_NOPE:], ((0, 0), (0, 0), (0, HEAD_PAD - MLA_V)))
    wkv_all = jnp.concatenate([wk.reshape(MLA_KV_LORA, -1), wv.reshape(MLA_KV_LORA, -1)], axis=1)
    return w_all, wq.astype(BF16), wkv_all.astype(BF16)


def _constants(tc):
    head = np.arange(LANES) // RWKV_HEAD_DIM
    hsum = (head[:, None] == head[None, :]).astype(np.float32)
    t = np.arange(tc)
    same = (t[:, None] // SUB) == (t[None, :] // SUB)
    cum = np.concatenate([same & (t[:, None] >= t[None, :]), same], axis=0).astype(np.float32)
    return {"hsum": jnp.asarray(hsum, BF16), "cum": jnp.asarray(cum, BF16)}


def kernel(x, positions, w_in, pre_norm, post_norm, rwkv_mu, rwkv_w0, rwkv_w_up, rwkv_a0, rwkv_a_up, rwkv_k_k, rwkv_k_a, rwkv_r_k, rwkv_ln_w, rwkv_ln_b, rwkv_v0, rwkv_v_down, rwkv_v_up, gmlp_ln_w, gmlp_ln_b, gmlp_w_s, gmlp_b_s, mla_q_norm, mla_w_uq, mla_kv_norm, mla_w_ukv, w_branch, w_out):
    bsz, seq, _ = x.shape
    n = bsz * seq
    depth = w_in.shape[0]
    tc = min(256, seq)
    tm = min(256, n)
    tq = min(512, seq)
    consts = _constants(tc)
    inv_freq = 1.0 / (ROPE_THETA ** (jnp.arange(0, MLA_ROPE, 2, dtype=F32) / MLA_ROPE))
    rope_tab = _rope_table(positions, inv_freq)
    spread = _rope_spread()

    x2 = x.reshape(n, D_MODEL)
    v_first = None
    for l in range(depth):
        w_all, wq, wkv = _layer_weights(l, w_in, rwkv_v_down, mla_w_uq, mla_w_ukv)
        ws = gmlp_w_s[l].reshape(GMLP_GROUPS // 2, 2, GMLP_CHUNK, GMLP_CHUNK)
        gprm = {"ln_w": _row(gmlp_ln_w[l]), "ln_b": _row(gmlp_ln_b[l]),
                "w_s": jnp.concatenate([ws[:, 0], ws[:, 1]], axis=-1),
                "bias": jnp.repeat(gmlp_b_s[l].T, GMLP_WIDTH // GMLP_GROUPS, axis=1)}
        o_rw, y_b, o_mla, o_mg = _in_proj(x2, _row(pre_norm[l]), w_all, gprm, tm)

        rprm = {"mu": _row(rwkv_mu[l]), "w0": _row(rwkv_w0[l]), "a0": _row(rwkv_a0[l]),
                "w_up": _pad_rows(rwkv_w_up[l], LANES, 0).astype(BF16),
                "a_up": _pad_rows(rwkv_a_up[l], LANES, RWKV_DECAY_LORA).astype(BF16),
                "k_k": _row(rwkv_k_k[l]), "k_a": _row(rwkv_k_a[l]), "r_k": _row(rwkv_r_k[l]),
                "ln_w": _row(rwkv_ln_w[l]), "ln_b": _row(rwkv_ln_b[l])}
        if l > 0:
            rprm["v0"] = _row(rwkv_v0[l - 1])
            rprm["v_up"] = _pad_rows(rwkv_v_up[l - 1], MISC, 0).astype(BF16)
        y_a, v_l = _rwkv(o_rw, v_first, rprm, consts, bsz, seq, tc)
        if l == 0:
            v_first = v_l

        mprm = {"spread": spread, "q_norm": _row(mla_q_norm[l]), "w_uq": wq,
                "kv_norm": _row(mla_kv_norm[l]), "w_ukv": wkv}
        q, k, vt = _mla_prep(o_mla, rope_tab, mprm, bsz, seq, tq)
        y_c = _attn(q, k, vt, o_mla, bsz, seq, tq, ATTN_HEADS_PER_STEP)

        x2 = _merge(y_a, y_b, y_c, o_mg, x2, w_branch[l].astype(BF16), w_out[l].astype(BF16),
                    _row(post_norm[l]), tm)
    return x2.reshape(bsz, seq, D_MODEL)
```

```python
import functools

import numpy as np
import jax
import jax.numpy as jnp
from jax import lax
from jax.experimental import pallas as pl
from jax.experimental.pallas import tpu as pltpu

F32 = jnp.float32
BF16 = jnp.bfloat16

D_MODEL = 1024
BRANCH_WIDTH = 512
N_BRANCHES = 3
NORM_EPS = 1e-6
RWKV_HEADS = 8
RWKV_HEAD_DIM = 64
RWKV_WIDTH = RWKV_HEADS * RWKV_HEAD_DIM
RWKV_DECAY_LORA = 64
RWKV_AAA_LORA = 64
RWKV_VRES_LORA = 32
RWKV_GN_EPS = 64e-5
RWKV_SHIFT_COLS = 3 * RWKV_WIDTH + RWKV_DECAY_LORA + RWKV_AAA_LORA
GMLP_WIDTH = 512
GMLP_GROUPS = 8
GMLP_CHUNK = 128
GMLP_LN_EPS = 1e-5
MLA_HEADS = 8
MLA_Q_LORA = 384
MLA_KV_LORA = 256
MLA_NOPE = 64
MLA_ROPE = 32
MLA_V = 64
MLA_WIDTH = MLA_HEADS * MLA_V
ROPE_THETA = 10000.0

LANES = 128
MISC = LANES
RW_COLS = RWKV_SHIFT_COLS + RWKV_WIDTH + MISC
G_COLS = 3 * GMLP_WIDTH
MLA_COLS = MLA_WIDTH + MLA_Q_LORA + MLA_KV_LORA + MISC
MG_COLS = N_BRANCHES * D_MODEL
ALL_COLS = RW_COLS + G_COLS + MLA_COLS + MG_COLS
HEAD_PAD = LANES
ROPE_LANE0 = MLA_NOPE
SUB = 64
PAIR = 2 * SUB

VMEM_LIMIT = 56 * 1024 * 1024
TIME_TILE = 256
ROW_TILE = 256
ATTN_TILE = 512


def _dot(a, b):
    return jnp.dot(a, b, preferred_element_type=F32)


def _dot_nt(a, b):
    return lax.dot_general(a, b, (((1,), (1,)), ((), ())), preferred_element_type=F32)


def _bdot(a, b):
    return lax.dot_general(a, b, (((2,), (1,)), ((0,), (0,))), preferred_element_type=F32)


def _bdot_nt(a, b):
    return lax.dot_general(a, b, (((2,), (2,)), ((0,), (0,))), preferred_element_type=F32)


def _bdot_tn(a, b):
    return lax.dot_general(a, b, (((1,), (1,)), ((0,), (0,))), preferred_element_type=F32)


def _split3(x):
    hi = x.astype(BF16)
    r1 = x - hi.astype(F32)
    mid = r1.astype(BF16)
    lo = (r1 - mid.astype(F32)).astype(BF16)
    return hi, mid, lo


def _dot_split2_lhs(m, x):
    hi = x.astype(BF16)
    lo = (x - hi.astype(F32)).astype(BF16)
    return _dot(m, hi) + _dot(m, lo)


def _dot_exact_rhs(x, m):
    hi, mid, lo = _split3(x)
    return _dot(hi, m) + _dot(mid, m) + _dot(lo, m)


def _head_sum(x, m):
    x16 = x.astype(BF16)
    return jnp.concatenate([_dot(x16[:, c:c + LANES], m) for c in range(0, x.shape[1], LANES)], axis=1)


def _sigmoid(x):
    return 0.5 * jnp.tanh(0.5 * x) + 0.5


def _const_spec(shape):
    nd = len(shape)
    return pl.BlockSpec(shape, lambda *_: (0,) * nd)


RW_PREP_OUTS = 7
PROJ_CHUNK = 512
PC_ROWS = 8
RW_PARAMS = ("mu", "w0", "w_up", "a0", "a_up", "k_k", "k_a", "r_k", "hsum", "cum")
RW_VRES_PARAMS = ("v0", "v_up")


def _in_proj_kernel(has_vres, tm, tiles_per_row, *refs):
    it = iter(refs)
    x_ref, g_ref, w_ref = next(it), next(it), next(it)
    vf_ref = next(it) if has_vres else None
    prm = {nm: next(it) for nm in RW_PARAMS + (RW_VRES_PARAMS if has_vres else ())}
    lnw_ref, lnb_ref, ws_ref, bias_ref = (next(it) for _ in range(4))
    at_o, rt_o, bt_o, kt_o, v_o, bd_o, kd_o, pc_o, bonus_o, gate_o = (next(it) for _ in range(RW_PREP_OUTS + 3))
    vout_o = None if has_vres else next(it)
    o_yb, o_mla, o_mg = next(it), next(it), next(it)
    carry_ref = next(it)

    x = x_ref[...]
    ms = jnp.mean(x * x, axis=-1, keepdims=True)
    h = (x * lax.rsqrt(ms + NORM_EPS) * g_ref[...]).astype(BF16)

    def columns(c0, c1):
        return _dot(h, w_ref[:, c0:c1])

    def project(o, col, c_from, c_to):
        for c0 in range(c_from, c_to, PROJ_CHUNK):
            cw = min(PROJ_CHUNK, c_to - c0)
            o[:, c0:c0 + cw] = columns(col + c0, col + c0 + cw).astype(o.dtype)

    p = jnp.concatenate([columns(c0, min(c0 + PROJ_CHUNK, RWKV_SHIFT_COLS))
                         for c0 in range(0, RWKV_SHIFT_COLS, PROJ_CHUNK)], axis=1)
    rw_gate = columns(RWKV_SHIFT_COLS, RWKV_SHIFT_COLS + RWKV_WIDTH)
    vd = columns(RWKV_SHIFT_COLS + RWKV_WIDTH, RW_COLS) if has_vres else None

    first_tile = (pl.program_id(0) % tiles_per_row) == 0
    row = lax.broadcasted_iota(jnp.int32, p.shape, 0)
    before = jnp.where(first_tile, 0.0, carry_ref[0:1, :])
    shifted = jnp.where(row == 0, before, pltpu.roll(p, 1, axis=0))
    carry_ref[0:1, :] = p[tm - 1:tm, :]
    p = p + (shifted - p) * prm["mu"][...]
    r = p[:, 0:RWKV_WIDTH]
    k = p[:, RWKV_WIDTH:2 * RWKV_WIDTH]
    v = p[:, 2 * RWKV_WIDTH:3 * RWKV_WIDTH]
    lora_in = p[:, 3 * RWKV_WIDTH:RWKV_SHIFT_COLS]
    lora_w = jnp.tanh(lora_in).astype(BF16)
    lora_a = lora_in.astype(BF16)
    hsum = prm["hsum"][...]
    kk = k * prm["k_k"][...]
    kk_sq = (kk * kk)

    gu, gv, gg = (columns(RW_COLS + i * GMLP_WIDTH, RW_COLS + (i + 1) * GMLP_WIDTH) for i in range(3))
    z = -(prm["w0"][...] + _dot(lora_w, prm["w_up"][...]))
    a_pre = prm["a0"][...] + _dot(lora_a, prm["a_up"][...])
    vg_pre = prm["v0"][...] + _dot(vd.astype(BF16), prm["v_up"][...]) if has_vres else None
    kk_norm2 = _head_sum(kk_sq, hsum)

    softplus = jnp.maximum(z, 0.0) + jnp.log1p(jnp.exp(-jnp.abs(z)))
    lw = -jnp.exp(-softplus - 0.5)
    a = _sigmoid(a_pre)
    if has_vres:
        v = v + (vf_ref[...] - v) * _sigmoid(vg_pre)
    else:
        vout_o[...] = v
    kk = kk * lax.rsqrt(jnp.maximum(kk_norm2, 1e-24))
    k2 = k * (1.0 + (a - 1.0) * prm["k_a"][...])
    av = -kk
    bv = kk * a
    rkk = r * k2 * prm["r_k"][...]
    gmlp_state = _gmlp_pointwise(gu, gv, gg, lnw_ref, lnb_ref)

    project(o_mla, RW_COLS + G_COLS, 0, MLA_COLS)
    gs = _dot_split2_lhs(prm["cum"][...], lw)
    bonus = _head_sum(rkk, hsum)
    _gmlp_position_mix(tm, gmlp_state, ws_ref, bias_ref, o_yb)

    project(o_mg, RW_COLS + G_COLS + MLA_COLS, 0, MG_COLS // 2)
    g = gs[0:tm]
    gl = gs[tm:2 * tm]
    e_neg = jnp.exp(-g)
    e_end = jnp.exp(gl - g)
    at_o[...] = (av * jnp.exp(g - lw)).astype(BF16)
    rt_o[...] = (r * jnp.exp(g)).astype(BF16)
    bt_o[...] = (bv * e_neg).astype(BF16)
    kt_o[...] = (k2 * e_neg).astype(BF16)
    v_o[...] = v.astype(BF16)
    bd_o[...] = (bv * e_end).astype(BF16)
    kd_o[...] = (k2 * e_end).astype(BF16)
    pc = jnp.exp(gl)
    nsub = tm // SUB
    pc_o[...] = jnp.concatenate([pc[SUB * c:SUB * c + 1] for c in range(nsub)]
                                + [jnp.zeros((PC_ROWS - nsub, RWKV_WIDTH), F32)], axis=0)
    bonus_o[...] = bonus * v
    gate_o[...] = (rw_gate * _sigmoid(rw_gate)).astype(BF16)
    project(o_mg, RW_COLS + G_COLS + MLA_COLS, MG_COLS // 2, MG_COLS)


def _in_proj(x2, gain, w_all, v_first, rprm, gprm, seq, tm):
    has_vres = v_first is not None
    n = x2.shape[0]
    rowspec = lambda c: pl.BlockSpec((tm, c), lambda i: (i, 0))
    args = [x2, gain, w_all]
    specs = [rowspec(D_MODEL), _const_spec((1, D_MODEL)),
             pl.BlockSpec((D_MODEL, ALL_COLS), lambda i: (0, 0), pipeline_mode=pl.Buffered(1))]
    if has_vres:
        args.append(v_first)
        specs.append(rowspec(RWKV_WIDTH))
    for nm in RW_PARAMS + (RW_VRES_PARAMS if has_vres else ()):
        args.append(rprm[nm])
        specs.append(_const_spec(rprm[nm].shape))
    for nm in ("ln_w", "ln_b", "w_s", "bias"):
        args.append(gprm[nm])
        specs.append(_const_spec(gprm[nm].shape))
    wide = lambda dt: (rowspec(RWKV_WIDTH), jax.ShapeDtypeStruct((n, RWKV_WIDTH), dt))
    outs = [wide(BF16) for _ in range(RW_PREP_OUTS)]
    outs.append((pl.BlockSpec((PC_ROWS, RWKV_WIDTH), lambda i: (i, 0)),
                 jax.ShapeDtypeStruct((n // tm * PC_ROWS, RWKV_WIDTH), F32)))
    outs += [wide(F32), wide(BF16)]
    if not has_vres:
        outs.append(wide(F32))
    outs += [(rowspec(c), jax.ShapeDtypeStruct((n, c), BF16)) for c in (GMLP_WIDTH, MLA_COLS, MG_COLS)]
    res = pl.pallas_call(
        functools.partial(_in_proj_kernel, has_vres, tm, seq // tm),
        grid=(n // tm,),
        in_specs=specs,
        out_specs=[o[0] for o in outs],
        out_shape=[o[1] for o in outs],
        scratch_shapes=[pltpu.VMEM((8, RWKV_SHIFT_COLS), F32)],
        compiler_params=pltpu.CompilerParams(dimension_semantics=("arbitrary",), vmem_limit_bytes=VMEM_LIMIT),
        name="in_proj_vres" if has_vres else "in_proj",
    )(*args)
    nrw = RW_PREP_OUTS + 3 + (0 if has_vres else 1)
    return res[:nrw], res[nrw], res[nrw + 1], res[nrw + 2]


def _rwkv_kernel(tc, *refs):
    it = iter(refs)
    prepared = [next(it) for _ in range(RW_PREP_OUTS)]
    pc_ref, bonus_ref, gate_ref, lnw_ref, lnb_ref, hsum_ref = (next(it) for _ in range(6))
    y_ref = next(it)
    state_ref = next(it)
    stacked = [next(it) for _ in range(RW_PREP_OUTS)]
    pc_s, y_s = next(it), next(it)

    @pl.when(pl.program_id(1) == 0)
    def _():
        state_ref[...] = jnp.zeros_like(state_ref)

    nsub = tc // SUB
    npair = RWKV_HEADS // 2

    def pieces(x):
        for c in range(nsub):
            for j in range(npair):
                yield c * npair + j, x[SUB * c:SUB * (c + 1), LANES * j:LANES * (j + 1)]

    lane = lax.broadcasted_iota(jnp.int32, (tc, RWKV_WIDTH), 1)
    first16 = jnp.where((lane % LANES) < RWKV_HEAD_DIM, 1.0, 0.0).astype(BF16)
    for src, dst in zip(prepared, stacked):
        x16 = src[...]
        xa = x16 * first16
        xb = x16 - xa
        for b, piece in pieces(xa):
            dst[b, 0:SUB, :] = piece
        for b, piece in pieces(xb):
            dst[b, SUB:PAIR, :] = piece
    pc = pc_ref[...]
    for c in range(nsub):
        for j in range(npair):
            pc_s[c * npair + j] = jnp.broadcast_to(pc[c:c + 1, LANES * j:LANES * (j + 1)], (PAIR, LANES))

    ri = lax.broadcasted_iota(jnp.int32, (PAIR, PAIR), 0)
    ci = lax.broadcasted_iota(jnp.int32, (PAIR, PAIR), 1)
    strict = ri > ci
    incl = ri >= ci
    eye = jnp.where(ri == ci, 1.0, 0.0).astype(F32)
    levels = (4, 8, 16, 32, 64)
    lvl_mask = [strict & ((ri // b) == (ci // b)) & ((ri // (b // 2)) != (ci // (b // 2))) for b in levels]
    lvl2 = strict & ((ri // 2) == (ci // 2))

    at, rt, bt, kt, vv, bd, kd = (s[...] for s in stacked)
    s = _bdot_nt(jnp.concatenate([at, rt], axis=1), jnp.concatenate([bt, kt], axis=1))
    nab = jnp.where(strict, s[:, 0:PAIR, 0:PAIR], 0.0)
    aak = jnp.where(strict, s[:, 0:PAIR, PAIR:2 * PAIR], 0.0).astype(BF16)
    rb = jnp.where(incl, s[:, PAIR:2 * PAIR, 0:PAIR], 0.0).astype(BF16)
    rk = jnp.where(incl, s[:, PAIR:2 * PAIR, PAIR:2 * PAIR], 0.0).astype(BF16)
    d = eye + jnp.where(lvl2, nab, 0.0)
    for m in lvl_mask:
        e = jnp.where(m, nab, 0.0).astype(BF16)
        db = d.astype(BF16)
        d = d + _bdot(_bdot(db, e).astype(BF16), db)
    t = d.astype(BF16)
    w1 = _bdot(aak, vv).astype(BF16)
    au = _bdot(t, jnp.concatenate([at, w1], axis=2))
    ah = au[:, :, 0:PAIR].astype(BF16)
    uh = au[:, :, PAIR:2 * PAIR]
    kv = _bdot_tn(kd, vv)
    pcm = jnp.swapaxes(pc_s[...], 1, 2)
    rbk = jnp.concatenate([rb, rk], axis=2)

    h = state_ref[...]
    for c in range(nsub):
        sl = slice(c * npair, (c + 1) * npair)
        arh = _bdot(jnp.concatenate([ah[sl], rt[sl]], axis=1), h.astype(BF16))
        u = (arh[:, 0:PAIR] + uh[sl]).astype(BF16)
        h = pcm[sl] * h + _bdot_tn(bd[sl], u) + kv[sl]
        y = arh[:, PAIR:2 * PAIR] + _bdot(rbk[sl], jnp.concatenate([u, vv[sl]], axis=1))
        for j in range(npair):
            y_s[SUB * c:SUB * (c + 1), LANES * j:LANES * (j + 1)] = y[j, 0:SUB] + y[j, SUB:PAIR]
    state_ref[...] = h

    y = y_s[...]
    hsum = hsum_ref[...]
    inv_n = 1.0 / RWKV_HEAD_DIM
    mean = _head_sum(y, hsum) * inv_n
    yc = y - mean
    var = _head_sum(yc * yc, hsum) * inv_n
    y = yc * lax.rsqrt(var + RWKV_GN_EPS) * lnw_ref[...] + lnb_ref[...]
    y_ref[...] = ((y + bonus_ref[...]) * gate_ref[...].astype(F32)).astype(y_ref.dtype)


def _rwkv(prepared, prm, bsz, seq, tc):
    n = bsz * seq
    nt = seq // tc
    row = lambda b, t: (b * nt + t, 0)
    wide = pl.BlockSpec((tc, RWKV_WIDTH), row)
    specs = [wide] * RW_PREP_OUTS + [pl.BlockSpec((PC_ROWS, RWKV_WIDTH), row), wide, wide]
    args = list(prepared[:RW_PREP_OUTS + 3])
    for nm in ("ln_w", "ln_b", "hsum"):
        args.append(prm[nm])
        specs.append(_const_spec(prm[nm].shape))
    nb = (tc // SUB) * (RWKV_HEADS // 2)
    scratch = [pltpu.VMEM((RWKV_HEADS // 2, PAIR, PAIR), F32)]
    scratch += [pltpu.VMEM((nb, PAIR, PAIR), BF16) for _ in range(RW_PREP_OUTS)]
    scratch += [pltpu.VMEM((nb, PAIR, PAIR), F32), pltpu.VMEM((tc, RWKV_WIDTH), F32)]
    return pl.pallas_call(
        functools.partial(_rwkv_kernel, tc),
        grid=(bsz, nt),
        in_specs=specs,
        out_specs=wide,
        out_shape=jax.ShapeDtypeStruct((n, RWKV_WIDTH), BF16),
        scratch_shapes=scratch,
        compiler_params=pltpu.CompilerParams(dimension_semantics=("arbitrary", "arbitrary"),
                                             vmem_limit_bytes=VMEM_LIMIT),
        name="rwkv",
    )(*args)


def _gelu(x):
    return 0.5 * x * (1.0 + lax.erf(x * np.float32(1.0 / np.sqrt(2.0))))


def _gmlp_pointwise(u, v, gate, lnw_ref, lnb_ref):
    u = _gelu(u)
    v = _gelu(v)
    mu = jnp.mean(v, axis=-1, keepdims=True)
    vc = v - mu
    var = jnp.mean(vc * vc, axis=-1, keepdims=True)
    v = vc * lax.rsqrt(var + GMLP_LN_EPS) * lnw_ref[...] + lnb_ref[...]
    return u * gate * _sigmoid(gate), v


def _gmlp_position_mix(tm, state, ws_ref, bias_ref, o_ref):
    ug, v = state
    lane = lax.broadcasted_iota(jnp.int32, (GMLP_CHUNK, LANES), 1)
    first = lane < (LANES // 2)
    ri = lax.broadcasted_iota(jnp.int32, (GMLP_CHUNK, 2 * GMLP_CHUNK), 0)
    ci = lax.broadcasted_iota(jnp.int32, (GMLP_CHUNK, 2 * GMLP_CHUNK), 1)
    causal = ri >= (ci % GMLP_CHUNK)
    for j in range(GMLP_GROUPS // 2):
        w = jnp.where(causal, ws_ref[j], 0.0).astype(BF16)
        ls = slice(LANES * j, LANES * (j + 1))
        for c in range(tm // GMLP_CHUNK):
            rs = slice(GMLP_CHUNK * c, GMLP_CHUNK * (c + 1))
            vp = v[rs, ls]
            stacked = jnp.concatenate([jnp.where(first, vp, 0.0), jnp.where(first, 0.0, vp)], axis=0).astype(BF16)
            vm = _dot(w, stacked) + bias_ref[:, ls]
            o_ref[rs, ls] = (ug[rs, ls] * vm).astype(o_ref.dtype)


def _rms(x, gain):
    return x * lax.rsqrt(jnp.mean(x * x, axis=-1, keepdims=True) + NORM_EPS) * gain


def _rope_table_kernel(pos_ref, invf_ref, o_ref):
    ang = pos_ref[...].astype(F32) * invf_ref[...]
    lane = lax.broadcasted_iota(jnp.int32, ang.shape, 1)
    o_ref[...] = jnp.where((lane % MLA_ROPE) < MLA_ROPE // 2, jnp.cos(ang), jnp.sin(ang))


def _rope_table(positions, inv_freq):
    n = positions.size
    per_row = LANES // MLA_ROPE
    pos = jnp.repeat(positions.reshape(n // per_row, per_row), MLA_ROPE, axis=1)
    invf = jnp.tile(inv_freq, LANES // inv_freq.size).reshape(1, LANES)
    rows = n // per_row
    tr = min(1024, rows)
    tab = pl.pallas_call(
        _rope_table_kernel,
        grid=(rows // tr,),
        in_specs=[pl.BlockSpec((tr, LANES), lambda i: (i, 0)), _const_spec((1, LANES))],
        out_specs=pl.BlockSpec((tr, LANES), lambda i: (i, 0)),
        out_shape=jax.ShapeDtypeStruct((rows, LANES), F32),
        name="rope_table",
    )(pos, invf)
    return tab.reshape(n, MLA_ROPE)


def _rope_spread():
    half = MLA_ROPE // 2
    m = np.zeros((MLA_ROPE, 2 * LANES), np.float32)
    for i in range(half):
        m[i, ROPE_LANE0 + i] = 1.0
        m[i, ROPE_LANE0 + half + i] = 1.0
        m[half + i, LANES + ROPE_LANE0 + i] = -1.0
        m[half + i, LANES + ROPE_LANE0 + half + i] = 1.0
    return jnp.asarray(m, BF16)


def _with_swapped_rope(w):
    half = MLA_ROPE // 2
    lo, hi = ROPE_LANE0, ROPE_LANE0 + MLA_ROPE
    return jnp.concatenate([w[..., :hi], w[..., lo + half:hi], w[..., lo:lo + half]], axis=-1)


def _mla_prep_kernel(m_ref, cs_ref, spread_ref, qn_ref, wq_ref, kvn_ref, wkv_ref, q_ref, k_ref, vt_ref):
    c0 = MLA_WIDTH
    c_q = m_ref[:, c0:c0 + MLA_Q_LORA].astype(F32)
    c_kv = m_ref[:, c0 + MLA_Q_LORA:c0 + MLA_Q_LORA + MLA_KV_LORA].astype(F32)
    k_rope = m_ref[:, c0 + MLA_Q_LORA + MLA_KV_LORA:MLA_COLS].astype(F32)
    tab = _dot_exact_rhs(cs_ref[...], spread_ref[...])
    lane = lax.broadcasted_iota(jnp.int32, (1, LANES), 1)
    cos = tab[:, 0:LANES] + jnp.where(lane < ROPE_LANE0, 1.0, 0.0)
    sin = tab[:, LANES:2 * LANES]

    def rope(x):
        return x * cos + pltpu.roll(x, LANES - MLA_ROPE, axis=1) * sin

    scale = np.float32((MLA_NOPE + MLA_ROPE) ** -0.5 * np.log2(np.e))
    q = _dot(_rms(c_q, qn_ref[...]).astype(BF16), wq_ref[...])
    kv = _dot(_rms(c_kv, kvn_ref[...]).astype(BF16), wkv_ref[...])
    kr = rope(k_rope)
    for h in range(MLA_HEADS):
        ls = slice(HEAD_PAD * h, HEAD_PAD * (h + 1))
        q_ref[:, ls] = (rope(q[:, ls]) * scale).astype(BF16)
        k_ref[:, ls] = (kv[:, ls] + kr).astype(BF16)
    vl = lax.broadcasted_iota(jnp.int32, (1, HEAD_PAD), 1)
    for h in range(MLA_HEADS):
        vh = kv[:, (MLA_HEADS + h) * HEAD_PAD:(MLA_HEADS + h + 1) * HEAD_PAD]
        vt_ref[0, h, 0] = jnp.where(vl == MLA_V, 1.0, vh).T.astype(BF16)


def _mla_prep(o_mla, rope_tab, prm, bsz, seq, tm):
    n = o_mla.shape[0]
    nt = seq // tm
    wide = MLA_HEADS * HEAD_PAD
    return pl.pallas_call(
        _mla_prep_kernel,
        grid=(n // tm,),
        in_specs=[pl.BlockSpec((tm, MLA_COLS), lambda i: (i, 0)),
                  pl.BlockSpec((tm, MLA_ROPE), lambda i: (i, 0)),
                  _const_spec(prm["spread"].shape),
                  _const_spec((1, MLA_Q_LORA)), _const_spec(prm["w_uq"].shape),
                  _const_spec((1, MLA_KV_LORA)), _const_spec(prm["w_ukv"].shape)],
        out_specs=[pl.BlockSpec((tm, wide), lambda i: (i, 0)), pl.BlockSpec((tm, wide), lambda i: (i, 0)),
                   pl.BlockSpec((1, MLA_HEADS, 1, HEAD_PAD, tm), lambda i: (i // nt, 0, i % nt, 0, 0))],
        out_shape=[jax.ShapeDtypeStruct((n, wide), BF16), jax.ShapeDtypeStruct((n, wide), BF16),
                   jax.ShapeDtypeStruct((bsz, MLA_HEADS, nt, HEAD_PAD, tm), BF16)],
        compiler_params=pltpu.CompilerParams(dimension_semantics=("arbitrary",), vmem_limit_bytes=VMEM_LIMIT),
        name="mla_prep",
    )(o_mla, rope_tab, prm["spread"], prm["q_norm"], prm["w_uq"], prm["kv_norm"], prm["w_ukv"])


ATTN_HEADS_PER_STEP = 4
V_ROWS = 80


def _attn_kernel(tq, nh, q_ref, k_ref, vt_ref, gate_ref, o_ref, m_s, acc_s):
    i = pl.program_id(2)
    ki = lax.broadcasted_iota(jnp.int32, (tq, tq), 0)
    qi = lax.broadcasted_iota(jnp.int32, (tq, tq), 1)
    causal = ki <= qi
    heads = [slice(HEAD_PAD * hh, HEAD_PAD * (hh + 1)) for hh in range(nh)]
    m_s[...] = jnp.full(m_s.shape, -jnp.inf, F32)
    acc_s[...] = jnp.zeros(acc_s.shape, F32)
    groups = tq // 8

    def block(j, masked):
        r0 = pl.multiple_of(j * tq, tq)
        sts = [_dot_nt(k_ref[pl.ds(r0, tq), ls], q_ref[:, ls]) for ls in heads]
        for hh, st in enumerate(sts):
            if masked:
                st = jnp.where(causal, st, -jnp.inf)
            s3 = st.reshape(groups, 8, tq)
            mc = jnp.max(s3, axis=0)
            for sh in (4, 2, 1):
                mc = jnp.maximum(mc, pltpu.roll(mc, sh, axis=0))
            m_old = m_s[hh]
            m_new = jnp.maximum(m_old, mc)
            p = jnp.exp2(s3 - m_new[None]).reshape(tq, tq).astype(BF16)
            alpha = jnp.exp2(m_old - m_new)
            pv = _dot(vt_ref[0, hh, j, 0:V_ROWS, :], p)
            acc_s[hh] = (acc_s[hh].reshape(V_ROWS // 8, 8, tq) * alpha[None]).reshape(V_ROWS, tq) + pv
            m_s[hh] = m_new

    def full_block(j, c):
        block(j, False)
        return c

    lax.fori_loop(0, i, full_block, 0)
    block(i, True)
    outs = []
    for hh in range(nh):
        acc = acc_s[hh]
        outs.append(acc[0:MLA_V] * (1.0 / acc[MLA_V:MLA_V + 1]))
    out = jnp.concatenate(outs, axis=0).T
    gate = gate_ref[...].astype(F32)
    o_ref[...] = (out * gate * _sigmoid(gate)).astype(o_ref.dtype)


def _attn(q, k, vt, o_mla, bsz, seq, tq, nh):
    n = bsz * seq
    nq = seq // tq
    return pl.pallas_call(
        functools.partial(_attn_kernel, tq, nh),
        grid=(bsz, MLA_HEADS // nh, nq),
        in_specs=[pl.BlockSpec((tq, nh * HEAD_PAD), lambda b, h, i: (b * nq + i, h)),
                  pl.BlockSpec((seq, nh * HEAD_PAD), lambda b, h, i: (b, h)),
                  pl.BlockSpec((1, nh, nq, HEAD_PAD, tq), lambda b, h, i: (b, h, 0, 0, 0)),
                  pl.BlockSpec((tq, nh * MLA_V), lambda b, h, i: (b * nq + i, h))],
        out_specs=pl.BlockSpec((tq, nh * MLA_V), lambda b, h, i: (b * nq + i, h)),
        out_shape=jax.ShapeDtypeStruct((n, MLA_WIDTH), BF16),
        scratch_shapes=[pltpu.VMEM((nh, 8, tq), F32), pltpu.VMEM((nh, V_ROWS, tq), F32)],
        compiler_params=pltpu.CompilerParams(dimension_semantics=("arbitrary",) * 3, vmem_limit_bytes=VMEM_LIMIT),
        name="attn",
    )(q, k, vt, o_mla)


def _merge_kernel(ya_ref, yb_ref, yc_ref, mg_ref, x_ref, wb_ref, wo_ref, pn_ref, o_ref):
    ys = (ya_ref[...], yb_ref[...], yc_ref[...])
    halves = []
    for c0 in range(0, D_MODEL, PROJ_CHUNK):
        acc = None
        for nb in range(N_BRANCHES):
            gates = _sigmoid(mg_ref[:, nb * D_MODEL + c0:nb * D_MODEL + c0 + PROJ_CHUNK].astype(F32))
            term = gates * _dot(ys[nb], wb_ref[nb, :, c0:c0 + PROJ_CHUNK])
            acc = term if acc is None else acc + term
        halves.append(acc.astype(BF16))
    mixed = _dot(jnp.concatenate(halves, axis=1), wo_ref[...])
    ms = jnp.mean(mixed * mixed, axis=-1, keepdims=True)
    o_ref[...] = x_ref[...] + mixed * lax.rsqrt(ms + NORM_EPS) * pn_ref[...]


def _merge(ya, yb, yc, o_mg, x2, w_branch, w_out, post_norm, tm):
    n = x2.shape[0]
    rowspec = lambda c: pl.BlockSpec((tm, c), lambda i: (i, 0))
    return pl.pallas_call(
        _merge_kernel,
        grid=(n // tm,),
        in_specs=[rowspec(BRANCH_WIDTH), rowspec(BRANCH_WIDTH), rowspec(BRANCH_WIDTH), rowspec(MG_COLS),
                  rowspec(D_MODEL), _const_spec(w_branch.shape), _const_spec(w_out.shape), _const_spec((1, D_MODEL))],
        out_specs=rowspec(D_MODEL),
        out_shape=jax.ShapeDtypeStruct((n, D_MODEL), F32),
        compiler_params=pltpu.CompilerParams(dimension_semantics=("arbitrary",), vmem_limit_bytes=VMEM_LIMIT),
        name="merge",
    )(ya, yb, yc, o_mg, x2, w_branch, w_out, post_norm)


def _row(v):
    return v.reshape(1, -1).astype(F32)


def _pad_rows(w, rows, at):
    out = jnp.zeros((rows, w.shape[1]), w.dtype)
    return out.at[at:at + w.shape[0]].set(w)


def _layer_weights(l, w_in, rwkv_v_down, mla_w_uq, mla_w_ukv):
    o = np.cumsum((0, RWKV_SHIFT_COLS, RWKV_WIDTH, GMLP_WIDTH, GMLP_WIDTH, GMLP_WIDTH,
                   MLA_Q_LORA, MLA_KV_LORA, MLA_ROPE, MLA_WIDTH, N_BRANCHES * D_MODEL))
    w = w_in[l]
    seg = lambda i: w[:, int(o[i]):int(o[i + 1])]
    misc_a = jnp.zeros((D_MODEL, MISC), F32)
    if l > 0:
        misc_a = misc_a.at[:, 0:RWKV_VRES_LORA].set(rwkv_v_down[l - 1])
    misc_b = _with_swapped_rope(jnp.zeros((D_MODEL, MISC), F32).at[:, ROPE_LANE0:ROPE_LANE0 + MLA_ROPE].set(seg(7)))
    w_all = jnp.concatenate([seg(0), seg(1), misc_a, seg(2), seg(3), seg(4),
                             seg(8), seg(5), seg(6), misc_b, seg(9)], axis=1).astype(BF16)
    qd = MLA_NOPE + MLA_ROPE
    wq = mla_w_uq[l].reshape(MLA_Q_LORA, MLA_HEADS, qd)
    wq = _with_swapped_rope(jnp.pad(wq, ((0, 0), (0, 0), (0, HEAD_PAD - qd)))).reshape(MLA_Q_LORA, MLA_HEADS * HEAD_PAD)
    wkv = mla_w_ukv[l].reshape(MLA_KV_LORA, MLA_HEADS, MLA_NOPE + MLA_V)
    wk = jnp.pad(wkv[:, :, :MLA_NOPE], ((0, 0), (0, 0), (0, HEAD_PAD - MLA_NOPE)))
    wv = jnp.pad(wkv[:, :, MLA_NOPE:], ((0, 0), (0, 0), (0, HEAD_PAD - MLA_V)))
    wkv_all = jnp.concatenate([wk.reshape(MLA_KV_LORA, -1), wv.reshape(MLA_KV_LORA, -1)], axis=1)
    return w_all, wq.astype(BF16), wkv_all.astype(BF16)


def _constants(tc):
    head = np.arange(LANES) // RWKV_HEAD_DIM
    hsum = (head[:, None] == head[None, :]).astype(np.float32)
    t = np.arange(tc)
    same = (t[:, None] // SUB) == (t[None, :] // SUB)
    cum = np.concatenate([same & (t[:, None] >= t[None, :]), same], axis=0).astype(np.float32)
    return {"hsum": jnp.asarray(hsum, BF16), "cum": jnp.asarray(cum, BF16)}


def kernel(x, positions, w_in, pre_norm, post_norm, rwkv_mu, rwkv_w0, rwkv_w_up, rwkv_a0, rwkv_a_up, rwkv_k_k, rwkv_k_a, rwkv_r_k, rwkv_ln_w, rwkv_ln_b, rwkv_v0, rwkv_v_down, rwkv_v_up, gmlp_ln_w, gmlp_ln_b, gmlp_w_s, gmlp_b_s, mla_q_norm, mla_w_uq, mla_kv_norm, mla_w_ukv, w_branch, w_out):
    bsz, seq, _ = x.shape
    n = bsz * seq
    depth = w_in.shape[0]
    tc, tm, tq = min(TIME_TILE, seq), min(ROW_TILE, n), min(ATTN_TILE, seq)
    consts = _constants(tc)
    inv_freq = 1.0 / (ROPE_THETA ** (jnp.arange(0, MLA_ROPE, 2, dtype=F32) / MLA_ROPE))
    rope_tab = _rope_table(positions, inv_freq)
    spread = _rope_spread()

    x2 = x.reshape(n, D_MODEL)
    v_first = None
    for l in range(depth):
        w_all, wq, wkv = _layer_weights(l, w_in, rwkv_v_down, mla_w_uq, mla_w_ukv)
        ws = gmlp_w_s[l].reshape(GMLP_GROUPS // 2, 2, GMLP_CHUNK, GMLP_CHUNK)
        gprm = {"ln_w": _row(gmlp_ln_w[l]), "ln_b": _row(gmlp_ln_b[l]),
                "w_s": jnp.concatenate([ws[:, 0], ws[:, 1]], axis=-1),
                "bias": jnp.repeat(gmlp_b_s[l].T, GMLP_WIDTH // GMLP_GROUPS, axis=1)}
        rprm = {"mu": _row(rwkv_mu[l]), "w0": _row(rwkv_w0[l]), "a0": _row(rwkv_a0[l]),
                "w_up": _pad_rows(rwkv_w_up[l], LANES, 0).astype(BF16),
                "a_up": _pad_rows(rwkv_a_up[l], LANES, RWKV_DECAY_LORA).astype(BF16),
                "k_k": _row(rwkv_k_k[l]), "k_a": _row(rwkv_k_a[l]), "r_k": _row(rwkv_r_k[l]),
                "ln_w": _row(rwkv_ln_w[l]), "ln_b": _row(rwkv_ln_b[l]), "hsum": consts["hsum"], "cum": consts["cum"]}
        if l > 0:
            rprm["v0"] = _row(rwkv_v0[l - 1])
            rprm["v_up"] = _pad_rows(rwkv_v_up[l - 1], MISC, 0).astype(BF16)
        prepared, y_b, o_mla, o_mg = _in_proj(x2, _row(pre_norm[l]), w_all, v_first, rprm, gprm, seq, tc)
        if l == 0:
            v_first = prepared[-1]
        y_a = _rwkv(prepared, rprm, bsz, seq, tc)

        mprm = {"spread": spread, "q_norm": _row(mla_q_norm[l]), "w_uq": wq,
                "kv_norm": _row(mla_kv_norm[l]), "w_ukv": wkv}
        q, k, vt = _mla_prep(o_mla, rope_tab, mprm, bsz, seq, tq)
        y_c = _attn(q, k, vt, o_mla, bsz, seq, tq, ATTN_HEADS_PER_STEP)

        x2 = _merge(y_a, y_b, y_c, o_mg, x2, w_branch[l].astype(BF16), w_out[l].astype(BF16),
                    _row(post_norm[l]), tm)
    return x2.reshape(bsz, seq, D_MODEL)
```

```python
import functools

import numpy as np
import jax
import jax.numpy as jnp
from jax import lax
from jax.experimental import pallas as pl
from jax.experimental.pallas import tpu as pltpu

F32 = jnp.float32
BF16 = jnp.bfloat16

D_MODEL = 1024
BRANCH_WIDTH = 512
N_BRANCHES = 3
NORM_EPS = 1e-6
RWKV_HEADS = 8
RWKV_HEAD_DIM = 64
RWKV_WIDTH = RWKV_HEADS * RWKV_HEAD_DIM
RWKV_DECAY_LORA = 64
RWKV_AAA_LORA = 64
RWKV_VRES_LORA = 32
RWKV_GN_EPS = 64e-5
RWKV_SHIFT_COLS = 3 * RWKV_WIDTH + RWKV_DECAY_LORA + RWKV_AAA_LORA
GMLP_WIDTH = 512
GMLP_GROUPS = 8
GMLP_CHUNK = 128
GMLP_LN_EPS = 1e-5
MLA_HEADS = 8
MLA_Q_LORA = 384
MLA_KV_LORA = 256
MLA_NOPE = 64
MLA_ROPE = 32
MLA_V = 64
MLA_WIDTH = MLA_HEADS * MLA_V
ROPE_THETA = 10000.0

LANES = 128
MISC = LANES
RW_COLS = RWKV_SHIFT_COLS + RWKV_WIDTH + MISC
G_COLS = 3 * GMLP_WIDTH
MLA_COLS = MLA_WIDTH + MLA_Q_LORA + MLA_KV_LORA + MISC
MG_COLS = N_BRANCHES * D_MODEL
ALL_COLS = RW_COLS + G_COLS + MLA_COLS + MG_COLS
HEAD_PAD = LANES
ROPE_LANE0 = MLA_NOPE
SUB = 64
PAIR = 2 * SUB

VMEM_LIMIT = 56 * 1024 * 1024
TIME_TILE = 256
ROW_TILE = 256
ATTN_TILE = 512


def _dot(a, b):
    return jnp.dot(a, b, preferred_element_type=F32)


def _dot_nt(a, b):
    return lax.dot_general(a, b, (((1,), (1,)), ((), ())), preferred_element_type=F32)


def _bdot(a, b):
    return lax.dot_general(a, b, (((2,), (1,)), ((0,), (0,))), preferred_element_type=F32)


def _bdot_nt(a, b):
    return lax.dot_general(a, b, (((2,), (2,)), ((0,), (0,))), preferred_element_type=F32)


def _bdot_tn(a, b):
    return lax.dot_general(a, b, (((1,), (1,)), ((0,), (0,))), preferred_element_type=F32)


def _split3(x):
    hi = x.astype(BF16)
    r1 = x - hi.astype(F32)
    mid = r1.astype(BF16)
    lo = (r1 - mid.astype(F32)).astype(BF16)
    return hi, mid, lo


def _dot_split2_lhs(m, x):
    hi = x.astype(BF16)
    lo = (x - hi.astype(F32)).astype(BF16)
    return _dot(m, hi) + _dot(m, lo)


def _dot_exact_rhs(x, m):
    hi, mid, lo = _split3(x)
    return _dot(hi, m) + _dot(mid, m) + _dot(lo, m)


def _head_sum(x, m):
    x16 = x.astype(BF16)
    return jnp.concatenate([_dot(x16[:, c:c + LANES], m) for c in range(0, x.shape[1], LANES)], axis=1)


def _sigmoid(x):
    return 0.5 * jnp.tanh(0.5 * x) + 0.5


def _const_spec(shape):
    nd = len(shape)
    return pl.BlockSpec(shape, lambda *_: (0,) * nd)


RW_PREP_OUTS = 7
PROJ_CHUNK = 512
PC_ROWS = 8
RW_PARAMS = ("mu", "w0", "w_up", "a0", "a_up", "k_k", "k_a", "r_k", "hsum", "cum")
RW_VRES_PARAMS = ("v0", "v_up")


def _in_proj_kernel(has_vres, tm, tiles_per_row, *refs):
    it = iter(refs)
    x_ref, g_ref, w_ref = next(it), next(it), next(it)
    vf_ref = next(it) if has_vres else None
    prm = {nm: next(it) for nm in RW_PARAMS + (RW_VRES_PARAMS if has_vres else ())}
    lnw_ref, lnb_ref, ws_ref, bias_ref = (next(it) for _ in range(4))
    at_o, rt_o, bt_o, kt_o, v_o, bd_o, kd_o, pc_o, bonus_o, gate_o = (next(it) for _ in range(RW_PREP_OUTS + 3))
    vout_o = None if has_vres else next(it)
    o_yb, o_mla, o_mg = next(it), next(it), next(it)
    carry_ref = next(it)

    x = x_ref[...]
    ms = jnp.mean(x * x, axis=-1, keepdims=True)
    h = (x * lax.rsqrt(ms + NORM_EPS) * g_ref[...]).astype(BF16)

    def columns(c0, c1):
        return _dot(h, w_ref[:, c0:c1])

    def project(o, col, c_from, c_to):
        for c0 in range(c_from, c_to, PROJ_CHUNK):
            cw = min(PROJ_CHUNK, c_to - c0)
            o[:, c0:c0 + cw] = columns(col + c0, col + c0 + cw).astype(o.dtype)

    p = jnp.concatenate([columns(c0, min(c0 + PROJ_CHUNK, RWKV_SHIFT_COLS))
                         for c0 in range(0, RWKV_SHIFT_COLS, PROJ_CHUNK)], axis=1)
    rw_gate = columns(RWKV_SHIFT_COLS, RWKV_SHIFT_COLS + RWKV_WIDTH)
    vd = columns(RWKV_SHIFT_COLS + RWKV_WIDTH, RW_COLS) if has_vres else None

    first_tile = (pl.program_id(0) % tiles_per_row) == 0
    row = lax.broadcasted_iota(jnp.int32, p.shape, 0)
    before = jnp.where(first_tile, 0.0, carry_ref[0:1, :])
    shifted = jnp.where(row == 0, before, pltpu.roll(p, 1, axis=0))
    carry_ref[0:1, :] = p[tm - 1:tm, :]
    p = p + (shifted - p) * prm["mu"][...]
    r = p[:, 0:RWKV_WIDTH]
    k = p[:, RWKV_WIDTH:2 * RWKV_WIDTH]
    v = p[:, 2 * RWKV_WIDTH:3 * RWKV_WIDTH]
    lora_in = p[:, 3 * RWKV_WIDTH:RWKV_SHIFT_COLS]
    lora_w = jnp.tanh(lora_in).astype(BF16)
    lora_a = lora_in.astype(BF16)
    hsum = prm["hsum"][...]
    kk = k * prm["k_k"][...]
    kk_sq = (kk * kk)

    gu, gv, gg = (columns(RW_COLS + i * GMLP_WIDTH, RW_COLS + (i + 1) * GMLP_WIDTH) for i in range(3))
    z = -(prm["w0"][...] + _dot(lora_w, prm["w_up"][...]))
    a_pre = prm["a0"][...] + _dot(lora_a, prm["a_up"][...])
    vg_pre = prm["v0"][...] + _dot(vd.astype(BF16), prm["v_up"][...]) if has_vres else None
    kk_norm2 = _head_sum(kk_sq, hsum)

    softplus = jnp.maximum(z, 0.0) + jnp.log1p(jnp.exp(-jnp.abs(z)))
    lw = -jnp.exp(-softplus - 0.5)
    a = _sigmoid(a_pre)
    if has_vres:
        v = v + (vf_ref[...] - v) * _sigmoid(vg_pre)
    else:
        vout_o[...] = v
    kk = kk * lax.rsqrt(jnp.maximum(kk_norm2, 1e-24))
    k2 = k * (1.0 + (a - 1.0) * prm["k_a"][...])
    av = -kk
    bv = kk * a
    rkk = r * k2 * prm["r_k"][...]
    gmlp_state = _gmlp_pointwise(gu, gv, gg, lnw_ref, lnb_ref)

    project(o_mla, RW_COLS + G_COLS, 0, MLA_COLS)
    gs = _dot_split2_lhs(prm["cum"][...], lw)
    bonus = _head_sum(rkk, hsum)
    _gmlp_position_mix(tm, gmlp_state, ws_ref, bias_ref, o_yb)

    project(o_mg, RW_COLS + G_COLS + MLA_COLS, 0, MG_COLS // 2)
    g = gs[0:tm]
    gl = gs[tm:2 * tm]
    e_neg = jnp.exp(-g)
    e_end = jnp.exp(gl - g)
    at_o[...] = (av * jnp.exp(g - lw)).astype(BF16)
    rt_o[...] = (r * jnp.exp(g)).astype(BF16)
    bt_o[...] = (bv * e_neg).astype(BF16)
    kt_o[...] = (k2 * e_neg).astype(BF16)
    v_o[...] = v.astype(BF16)
    bd_o[...] = (bv * e_end).astype(BF16)
    kd_o[...] = (k2 * e_end).astype(BF16)
    pc = jnp.exp(gl)
    nsub = tm // SUB
    pc_o[...] = jnp.concatenate([pc[SUB * c:SUB * c + 1] for c in range(nsub)]
                                + [jnp.zeros((PC_ROWS - nsub, RWKV_WIDTH), F32)], axis=0)
    bonus_o[...] = bonus * v
    gate_o[...] = (rw_gate * _sigmoid(rw_gate)).astype(BF16)
    project(o_mg, RW_COLS + G_COLS + MLA_COLS, MG_COLS // 2, MG_COLS)


def _in_proj(x2, gain, w_all, v_first, rprm, gprm, seq, tm):
    has_vres = v_first is not None
    n = x2.shape[0]
    rowspec = lambda c: pl.BlockSpec((tm, c), lambda i: (i, 0))
    args = [x2, gain, w_all]
    specs = [rowspec(D_MODEL), _const_spec((1, D_MODEL)),
             pl.BlockSpec((D_MODEL, ALL_COLS), lambda i: (0, 0), pipeline_mode=pl.Buffered(1))]
    if has_vres:
        args.append(v_first)
        specs.append(rowspec(RWKV_WIDTH))
    for nm in RW_PARAMS + (RW_VRES_PARAMS if has_vres else ()):
        args.append(rprm[nm])
        specs.append(_const_spec(rprm[nm].shape))
    for nm in ("ln_w", "ln_b", "w_s", "bias"):
        args.append(gprm[nm])
        specs.append(_const_spec(gprm[nm].shape))
    wide = lambda dt: (rowspec(RWKV_WIDTH), jax.ShapeDtypeStruct((n, RWKV_WIDTH), dt))
    outs = [wide(BF16) for _ in range(RW_PREP_OUTS)]
    outs.append((pl.BlockSpec((PC_ROWS, RWKV_WIDTH), lambda i: (i, 0)),
                 jax.ShapeDtypeStruct((n // tm * PC_ROWS, RWKV_WIDTH), F32)))
    outs += [wide(F32), wide(BF16)]
    if not has_vres:
        outs.append(wide(F32))
    outs += [(rowspec(c), jax.ShapeDtypeStruct((n, c), BF16)) for c in (GMLP_WIDTH, MLA_COLS, MG_COLS)]
    res = pl.pallas_call(
        functools.partial(_in_proj_kernel, has_vres, tm, seq // tm),
        grid=(n // tm,),
        in_specs=specs,
        out_specs=[o[0] for o in outs],
        out_shape=[o[1] for o in outs],
        scratch_shapes=[pltpu.VMEM((8, RWKV_SHIFT_COLS), F32)],
        compiler_params=pltpu.CompilerParams(dimension_semantics=("arbitrary",), vmem_limit_bytes=VMEM_LIMIT),
        name="in_proj_vres" if has_vres else "in_proj",
    )(*args)
    nrw = RW_PREP_OUTS + 3 + (0 if has_vres else 1)
    return res[:nrw], res[nrw], res[nrw + 1], res[nrw + 2]


def _rwkv_kernel(tc, *refs):
    it = iter(refs)
    prepared = [next(it) for _ in range(RW_PREP_OUTS)]
    pc_ref, bonus_ref, gate_ref, lnw_ref, lnb_ref, hsum_ref = (next(it) for _ in range(6))
    y_ref = next(it)
    state_ref = next(it)
    stacked = [next(it) for _ in range(RW_PREP_OUTS)]
    pc_s, y_s = next(it), next(it)

    @pl.when(pl.program_id(1) == 0)
    def _():
        state_ref[...] = jnp.zeros_like(state_ref)

    nsub = tc // SUB
    npair = RWKV_HEADS // 2

    def pieces(x):
        for c in range(nsub):
            for j in range(npair):
                yield c * npair + j, x[SUB * c:SUB * (c + 1), LANES * j:LANES * (j + 1)]

    lane = lax.broadcasted_iota(jnp.int32, (tc, RWKV_WIDTH), 1)
    first16 = jnp.where((lane % LANES) < RWKV_HEAD_DIM, 1.0, 0.0).astype(BF16)
    for src, dst in zip(prepared, stacked):
        x16 = src[...]
        xa = x16 * first16
        xb = x16 - xa
        for b, piece in pieces(xa):
            dst[b, 0:SUB, :] = piece
        for b, piece in pieces(xb):
            dst[b, SUB:PAIR, :] = piece
    pc = pc_ref[...]
    for c in range(nsub):
        for j in range(npair):
            pc_s[c * npair + j] = jnp.broadcast_to(pc[c:c + 1, LANES * j:LANES * (j + 1)], (PAIR, LANES))

    ri = lax.broadcasted_iota(jnp.int32, (PAIR, PAIR), 0)
    ci = lax.broadcasted_iota(jnp.int32, (PAIR, PAIR), 1)
    strict = ri > ci
    incl = ri >= ci
    eye = jnp.where(ri == ci, 1.0, 0.0).astype(F32)
    levels = (4, 8, 16, 32, 64)
    lvl_mask = [strict & ((ri // b) == (ci // b)) & ((ri // (b // 2)) != (ci // (b // 2))) for b in levels]
    lvl2 = strict & ((ri // 2) == (ci // 2))

    at, rt, bt, kt, vv, bd, kd = (s[...] for s in stacked)
    s = _bdot_nt(jnp.concatenate([at, rt], axis=1), jnp.concatenate([bt, kt], axis=1))
    nab = jnp.where(strict, s[:, 0:PAIR, 0:PAIR], 0.0)
    aak = jnp.where(strict, s[:, 0:PAIR, PAIR:2 * PAIR], 0.0).astype(BF16)
    rb = jnp.where(incl, s[:, PAIR:2 * PAIR, 0:PAIR], 0.0).astype(BF16)
    rk = jnp.where(incl, s[:, PAIR:2 * PAIR, PAIR:2 * PAIR], 0.0).astype(BF16)
    d = eye + jnp.where(lvl2, nab, 0.0)
    for m in lvl_mask:
        e = jnp.where(m, nab, 0.0).astype(BF16)
        db = d.astype(BF16)
        d = d + _bdot(_bdot(db, e).astype(BF16), db)
    t = d.astype(BF16)
    w1 = _bdot(aak, vv).astype(BF16)
    au = _bdot(t, jnp.concatenate([at, w1], axis=2))
    ah = au[:, :, 0:PAIR].astype(BF16)
    uh = au[:, :, PAIR:2 * PAIR]
    kv = _bdot_tn(kd, vv)
    pcm = jnp.swapaxes(pc_s[...], 1, 2)
    rbk = jnp.concatenate([rb, rk], axis=2)

    h = state_ref[...]
    for c in range(nsub):
        sl = slice(c * npair, (c + 1) * npair)
        arh = _bdot(jnp.concatenate([ah[sl], rt[sl]], axis=1), h.astype(BF16))
        u = (arh[:, 0:PAIR] + uh[sl]).astype(BF16)
        h = pcm[sl] * h + _bdot_tn(bd[sl], u) + kv[sl]
        y = arh[:, PAIR:2 * PAIR] + _bdot(rbk[sl], jnp.concatenate([u, vv[sl]], axis=1))
        for j in range(npair):
            y_s[SUB * c:SUB * (c + 1), LANES * j:LANES * (j + 1)] = y[j, 0:SUB] + y[j, SUB:PAIR]
    state_ref[...] = h

    y = y_s[...]
    hsum = hsum_ref[...]
    inv_n = 1.0 / RWKV_HEAD_DIM
    mean = _head_sum(y, hsum) * inv_n
    yc = y - mean
    var = _head_sum(yc * yc, hsum) * inv_n
    y = yc * lax.rsqrt(var + RWKV_GN_EPS) * lnw_ref[...] + lnb_ref[...]
    y_ref[...] = ((y + bonus_ref[...]) * gate_ref[...].astype(F32)).astype(y_ref.dtype)


def _rwkv(prepared, prm, bsz, seq, tc):
    n = bsz * seq
    nt = seq // tc
    row = lambda b, t: (b * nt + t, 0)
    wide = pl.BlockSpec((tc, RWKV_WIDTH), row)
    specs = [wide] * RW_PREP_OUTS + [pl.BlockSpec((PC_ROWS, RWKV_WIDTH), row), wide, wide]
    args = list(prepared[:RW_PREP_OUTS + 3])
    for nm in ("ln_w", "ln_b", "hsum"):
        args.append(prm[nm])
        specs.append(_const_spec(prm[nm].shape))
    nb = (tc // SUB) * (RWKV_HEADS // 2)
    scratch = [pltpu.VMEM((RWKV_HEADS // 2, PAIR, PAIR), F32)]
    scratch += [pltpu.VMEM((nb, PAIR, PAIR), BF16) for _ in range(RW_PREP_OUTS)]
    scratch += [pltpu.VMEM((nb, PAIR, PAIR), F32), pltpu.VMEM((tc, RWKV_WIDTH), F32)]
    return pl.pallas_call(
        functools.partial(_rwkv_kernel, tc),
        grid=(bsz, nt),
        in_specs=specs,
        out_specs=wide,
        out_shape=jax.ShapeDtypeStruct((n, RWKV_WIDTH), BF16),
        scratch_shapes=scratch,
        compiler_params=pltpu.CompilerParams(dimension_semantics=("arbitrary", "arbitrary"),
                                             vmem_limit_bytes=VMEM_LIMIT),
        name="rwkv",
    )(*args)


def _gelu(x):
    return 0.5 * x * (1.0 + lax.erf(x * np.float32(1.0 / np.sqrt(2.0))))


def _gmlp_pointwise(u, v, gate, lnw_ref, lnb_ref):
    u = _gelu(u)
    v = _gelu(v)
    mu = jnp.mean(v, axis=-1, keepdims=True)
    vc = v - mu
    var = jnp.mean(vc * vc, axis=-1, keepdims=True)
    v = vc * lax.rsqrt(var + GMLP_LN_EPS) * lnw_ref[...] + lnb_ref[...]
    return u * gate * _sigmoid(gate), v


def _gmlp_position_mix(tm, state, ws_ref, bias_ref, o_ref):
    ug, v = state
    lane = lax.broadcasted_iota(jnp.int32, (GMLP_CHUNK, LANES), 1)
    first = lane < (LANES // 2)
    ri = lax.broadcasted_iota(jnp.int32, (GMLP_CHUNK, 2 * GMLP_CHUNK), 0)
    ci = lax.broadcasted_iota(jnp.int32, (GMLP_CHUNK, 2 * GMLP_CHUNK), 1)
    causal = ri >= (ci % GMLP_CHUNK)
    for j in range(GMLP_GROUPS // 2):
        w = jnp.where(causal, ws_ref[j], 0.0).astype(BF16)
        ls = slice(LANES * j, LANES * (j + 1))
        for c in range(tm // GMLP_CHUNK):
            rs = slice(GMLP_CHUNK * c, GMLP_CHUNK * (c + 1))
            vp = v[rs, ls]
            stacked = jnp.concatenate([jnp.where(first, vp, 0.0), jnp.where(first, 0.0, vp)], axis=0).astype(BF16)
            vm = _dot(w, stacked) + bias_ref[:, ls]
            o_ref[rs, ls] = (ug[rs, ls] * vm).astype(o_ref.dtype)


def _rms(x, gain):
    return x * lax.rsqrt(jnp.mean(x * x, axis=-1, keepdims=True) + NORM_EPS) * gain


def _rope_table_kernel(pos_ref, invf_ref, o_ref):
    ang = pos_ref[...].astype(F32) * invf_ref[...]
    lane = lax.broadcasted_iota(jnp.int32, ang.shape, 1)
    o_ref[...] = jnp.where((lane % MLA_ROPE) < MLA_ROPE // 2, jnp.cos(ang), jnp.sin(ang))


def _rope_table(positions, inv_freq):
    n = positions.size
    per_row = LANES // MLA_ROPE
    pos = jnp.repeat(positions.reshape(n // per_row, per_row), MLA_ROPE, axis=1)
    invf = jnp.tile(inv_freq, LANES // inv_freq.size).reshape(1, LANES)
    rows = n // per_row
    tr = min(1024, rows)
    tab = pl.pallas_call(
        _rope_table_kernel,
        grid=(rows // tr,),
        in_specs=[pl.BlockSpec((tr, LANES), lambda i: (i, 0)), _const_spec((1, LANES))],
        out_specs=pl.BlockSpec((tr, LANES), lambda i: (i, 0)),
        out_shape=jax.ShapeDtypeStruct((rows, LANES), F32),
        name="rope_table",
    )(pos, invf)
    return tab.reshape(n, MLA_ROPE)


def _rope_spread():
    half = MLA_ROPE // 2
    m = np.zeros((MLA_ROPE, 2 * LANES), np.float32)
    for i in range(half):
        m[i, ROPE_LANE0 + i] = 1.0
        m[i, ROPE_LANE0 + half + i] = 1.0
        m[half + i, LANES + ROPE_LANE0 + i] = -1.0
        m[half + i, LANES + ROPE_LANE0 + half + i] = 1.0
    return jnp.asarray(m, BF16)


def _with_swapped_rope(w):
    half = MLA_ROPE // 2
    lo, hi = ROPE_LANE0, ROPE_LANE0 + MLA_ROPE
    return jnp.concatenate([w[..., :hi], w[..., lo + half:hi], w[..., lo:lo + half]], axis=-1)


def _mla_prep_kernel(m_ref, cs_ref, spread_ref, qn_ref, wq_ref, kvn_ref, wkv_ref, q_ref, k_ref, vt_ref):
    c0 = MLA_WIDTH
    c_q = m_ref[:, c0:c0 + MLA_Q_LORA].astype(F32)
    c_kv = m_ref[:, c0 + MLA_Q_LORA:c0 + MLA_Q_LORA + MLA_KV_LORA].astype(F32)
    k_rope = m_ref[:, c0 + MLA_Q_LORA + MLA_KV_LORA:MLA_COLS].astype(F32)
    tab = _dot_exact_rhs(cs_ref[...], spread_ref[...])
    lane = lax.broadcasted_iota(jnp.int32, (1, LANES), 1)
    cos = tab[:, 0:LANES] + jnp.where(lane < ROPE_LANE0, 1.0, 0.0)
    sin = tab[:, LANES:2 * LANES]

    def rope(x):
        return x * cos + pltpu.roll(x, LANES - MLA_ROPE, axis=1) * sin

    scale = np.float32((MLA_NOPE + MLA_ROPE) ** -0.5 * np.log2(np.e))
    q = _dot(_rms(c_q, qn_ref[...]).astype(BF16), wq_ref[...])
    kv = _dot(_rms(c_kv, kvn_ref[...]).astype(BF16), wkv_ref[...])
    kr = rope(k_rope)
    for h in range(MLA_HEADS):
        ls = slice(HEAD_PAD * h, HEAD_PAD * (h + 1))
        q_ref[:, ls] = (rope(q[:, ls]) * scale).astype(BF16)
        k_ref[:, ls] = (kv[:, ls] + kr).astype(BF16)
    vl = lax.broadcasted_iota(jnp.int32, (1, HEAD_PAD), 1)
    for h in range(MLA_HEADS):
        vh = kv[:, (MLA_HEADS + h) * HEAD_PAD:(MLA_HEADS + h + 1) * HEAD_PAD]
        vt_ref[0, h, 0] = jnp.where(vl == MLA_V, 1.0, vh).T.astype(BF16)


def _mla_prep(o_mla, rope_tab, prm, bsz, seq, tm):
    n = o_mla.shape[0]
    nt = seq // tm
    wide = MLA_HEADS * HEAD_PAD
    return pl.pallas_call(
        _mla_prep_kernel,
        grid=(n // tm,),
        in_specs=[pl.BlockSpec((tm, MLA_COLS), lambda i: (i, 0)),
                  pl.BlockSpec((tm, MLA_ROPE), lambda i: (i, 0)),
                  _const_spec(prm["spread"].shape),
                  _const_spec((1, MLA_Q_LORA)), _const_spec(prm["w_uq"].shape),
                  _const_spec((1, MLA_KV_LORA)), _const_spec(prm["w_ukv"].shape)],
        out_specs=[pl.BlockSpec((tm, wide), lambda i: (i, 0)), pl.BlockSpec((tm, wide), lambda i: (i, 0)),
                   pl.BlockSpec((1, MLA_HEADS, 1, HEAD_PAD, tm), lambda i: (i // nt, 0, i % nt, 0, 0))],
        out_shape=[jax.ShapeDtypeStruct((n, wide), BF16), jax.ShapeDtypeStruct((n, wide), BF16),
                   jax.ShapeDtypeStruct((bsz, MLA_HEADS, nt, HEAD_PAD, tm), BF16)],
        compiler_params=pltpu.CompilerParams(dimension_semantics=("arbitrary",), vmem_limit_bytes=VMEM_LIMIT),
        name="mla_prep",
    )(o_mla, rope_tab, prm["spread"], prm["q_norm"], prm["w_uq"], prm["kv_norm"], prm["w_ukv"])


ATTN_HEADS_PER_STEP = 4
V_ROWS = 80


def _attn_kernel(tq, nh, q_ref, k_ref, vt_ref, gate_ref, o_ref, m_s, acc_s):
    i = pl.program_id(2)
    ki = lax.broadcasted_iota(jnp.int32, (tq, tq), 0)
    qi = lax.broadcasted_iota(jnp.int32, (tq, tq), 1)
    causal = ki <= qi
    heads = [slice(HEAD_PAD * hh, HEAD_PAD * (hh + 1)) for hh in range(nh)]
    m_s[...] = jnp.full(m_s.shape, -jnp.inf, F32)
    acc_s[...] = jnp.zeros(acc_s.shape, F32)
    groups = tq // 8

    def block(j, masked):
        r0 = pl.multiple_of(j * tq, tq)
        sts = [_dot_nt(k_ref[pl.ds(r0, tq), ls], q_ref[:, ls]) for ls in heads]
        for hh, st in enumerate(sts):
            if masked:
                st = jnp.where(causal, st, -jnp.inf)
            s3 = st.reshape(groups, 8, tq)
            mc = jnp.max(s3, axis=0)
            for sh in (4, 2, 1):
                mc = jnp.maximum(mc, pltpu.roll(mc, sh, axis=0))
            m_old = m_s[hh]
            m_new = jnp.maximum(m_old, mc)
            p = jnp.exp2(s3 - m_new[None]).reshape(tq, tq).astype(BF16)
            alpha = jnp.exp2(m_old - m_new)
            pv = _dot(vt_ref[0, hh, j, 0:V_ROWS, :], p)
            acc_s[hh] = (acc_s[hh].reshape(V_ROWS // 8, 8, tq) * alpha[None]).reshape(V_ROWS, tq) + pv
            m_s[hh] = m_new

    def full_block(j, c):
        block(j, False)
        return c

    lax.fori_loop(0, i, full_block, 0)
    block(i, True)
    outs = []
    for hh in range(nh):
        acc = acc_s[hh]
        outs.append(acc[0:MLA_V] * (1.0 / acc[MLA_V:MLA_V + 1]))
    out = jnp.concatenate(outs, axis=0).T
    gate = gate_ref[...].astype(F32)
    o_ref[...] = (out * gate * _sigmoid(gate)).astype(o_ref.dtype)


def _attn(q, k, vt, o_mla, bsz, seq, tq, nh):
    n = bsz * seq
    nq = seq // tq
    return pl.pallas_call(
        functools.partial(_attn_kernel, tq, nh),
        grid=(bsz, MLA_HEADS // nh, nq),
        in_specs=[pl.BlockSpec((tq, nh * HEAD_PAD), lambda b, h, i: (b * nq + i, h)),
                  pl.BlockSpec((seq, nh * HEAD_PAD), lambda b, h, i: (b, h)),
                  pl.BlockSpec((1, nh, nq, HEAD_PAD, tq), lambda b, h, i: (b, h, 0, 0, 0)),
                  pl.BlockSpec((tq, nh * MLA_V), lambda b, h, i: (b * nq + i, h))],
        out_specs=pl.BlockSpec((tq, nh * MLA_V), lambda b, h, i: (b * nq + i, h)),
        out_shape=jax.ShapeDtypeStruct((n, MLA_WIDTH), BF16),
        scratch_shapes=[pltpu.VMEM((nh, 8, tq), F32), pltpu.VMEM((nh, V_ROWS, tq), F32)],
        compiler_params=pltpu.CompilerParams(dimension_semantics=("arbitrary",) * 3, vmem_limit_bytes=VMEM_LIMIT),
        name="attn",
    )(q, k, vt, o_mla)


def _merge_kernel(ya_ref, yb_ref, yc_ref, mg_ref, x_ref, wb_ref, wo_ref, pn_ref, o_ref):
    ys = (ya_ref[...], yb_ref[...], yc_ref[...])
    halves = []
    for c0 in range(0, D_MODEL, PROJ_CHUNK):
        acc = None
        for nb in range(N_BRANCHES):
            gates = _sigmoid(mg_ref[:, nb * D_MODEL + c0:nb * D_MODEL + c0 + PROJ_CHUNK].astype(F32))
            term = gates * _dot(ys[nb], wb_ref[nb, :, c0:c0 + PROJ_CHUNK])
            acc = term if acc is None else acc + term
        halves.append(acc.astype(BF16))
    mixed = _dot(jnp.concatenate(halves, axis=1), wo_ref[...])
    ms = jnp.mean(mixed * mixed, axis=-1, keepdims=True)
    o_ref[...] = x_ref[...] + mixed * lax.rsqrt(ms + NORM_EPS) * pn_ref[...]


def _merge(ya, yb, yc, o_mg, x2, w_branch, w_out, post_norm, tm):
    n = x2.shape[0]
    rowspec = lambda c: pl.BlockSpec((tm, c), lambda i: (i, 0))
    return pl.pallas_call(
        _merge_kernel,
        grid=(n // tm,),
        in_specs=[rowspec(BRANCH_WIDTH), rowspec(BRANCH_WIDTH), rowspec(BRANCH_WIDTH), rowspec(MG_COLS),
                  rowspec(D_MODEL), _const_spec(w_branch.shape), _const_spec(w_out.shape), _const_spec((1, D_MODEL))],
        out_specs=rowspec(D_MODEL),
        out_shape=jax.ShapeDtypeStruct((n, D_MODEL), F32),
        compiler_params=pltpu.CompilerParams(dimension_semantics=("arbitrary",), vmem_limit_bytes=VMEM_LIMIT),
        name="merge",
    )(ya, yb, yc, o_mg, x2, w_branch, w_out, post_norm)


def _row(v):
    return v.reshape(1, -1).astype(F32)


def _pad_rows(w, rows, at):
    out = jnp.zeros((rows, w.shape[1]), w.dtype)
    return out.at[at:at + w.shape[0]].set(w)


def _layer_weights(l, w_in, rwkv_v_down, mla_w_uq, mla_w_ukv):
    o = np.cumsum((0, RWKV_SHIFT_COLS, RWKV_WIDTH, GMLP_WIDTH, GMLP_WIDTH, GMLP_WIDTH,
                   MLA_Q_LORA, MLA_KV_LORA, MLA_ROPE, MLA_WIDTH, N_BRANCHES * D_MODEL))
    w = w_in[l].astype(BF16)
    seg = lambda i: w[:, int(o[i]):int(o[i + 1])]
    misc_a = jnp.zeros((D_MODEL, MISC), BF16)
    if l > 0:
        misc_a = misc_a.at[:, 0:RWKV_VRES_LORA].set(rwkv_v_down[l - 1].astype(BF16))
    misc_b = _with_swapped_rope(jnp.zeros((D_MODEL, MISC), BF16).at[:, ROPE_LANE0:ROPE_LANE0 + MLA_ROPE].set(seg(7)))
    w_all = jnp.concatenate([w[:, 0:int(o[2])], misc_a, w[:, int(o[2]):int(o[5])],
                             seg(8), seg(5), seg(6), misc_b, seg(9)], axis=1)
    qd = MLA_NOPE + MLA_ROPE
    wq = mla_w_uq[l].reshape(MLA_Q_LORA, MLA_HEADS, qd)
    wq = _with_swapped_rope(jnp.pad(wq, ((0, 0), (0, 0), (0, HEAD_PAD - qd)))).reshape(MLA_Q_LORA, MLA_HEADS * HEAD_PAD)
    wkv = mla_w_ukv[l].reshape(MLA_KV_LORA, MLA_HEADS, MLA_NOPE + MLA_V)
    wk = jnp.pad(wkv[:, :, :MLA_NOPE], ((0, 0), (0, 0), (0, HEAD_PAD - MLA_NOPE)))
    wv = jnp.pad(wkv[:, :, MLA_NOPE:], ((0, 0), (0, 0), (0, HEAD_PAD - MLA_V)))
    wkv_all = jnp.concatenate([wk.reshape(MLA_KV_LORA, -1), wv.reshape(MLA_KV_LORA, -1)], axis=1)
    return w_all, wq.astype(BF16), wkv_all.astype(BF16)


def _constants(tc):
    head = np.arange(LANES) // RWKV_HEAD_DIM
    hsum = (head[:, None] == head[None, :]).astype(np.float32)
    t = np.arange(tc)
    same = (t[:, None] // SUB) == (t[None, :] // SUB)
    cum = np.concatenate([same & (t[:, None] >= t[None, :]), same], axis=0).astype(np.float32)
    return {"hsum": jnp.asarray(hsum, BF16), "cum": jnp.asarray(cum, BF16)}


def kernel(x, positions, w_in, pre_norm, post_norm, rwkv_mu, rwkv_w0, rwkv_w_up, rwkv_a0, rwkv_a_up, rwkv_k_k, rwkv_k_a, rwkv_r_k, rwkv_ln_w, rwkv_ln_b, rwkv_v0, rwkv_v_down, rwkv_v_up, gmlp_ln_w, gmlp_ln_b, gmlp_w_s, gmlp_b_s, mla_q_norm, mla_w_uq, mla_kv_norm, mla_w_ukv, w_branch, w_out):
    bsz, seq, _ = x.shape
    n = bsz * seq
    depth = w_in.shape[0]
    tc, tm, tq = min(TIME_TILE, seq), min(ROW_TILE, n), min(ATTN_TILE, seq)
    consts = _constants(tc)
    inv_freq = 1.0 / (ROPE_THETA ** (jnp.arange(0, MLA_ROPE, 2, dtype=F32) / MLA_ROPE))
    rope_tab = _rope_table(positions, inv_freq)
    spread = _rope_spread()

    x2 = x.reshape(n, D_MODEL)
    v_first = None
    for l in range(depth):
        w_all, wq, wkv = _layer_weights(l, w_in, rwkv_v_down, mla_w_uq, mla_w_ukv)
        ws = gmlp_w_s[l].reshape(GMLP_GROUPS // 2, 2, GMLP_CHUNK, GMLP_CHUNK)
        gprm = {"ln_w": _row(gmlp_ln_w[l]), "ln_b": _row(gmlp_ln_b[l]),
                "w_s": jnp.concatenate([ws[:, 0], ws[:, 1]], axis=-1),
                "bias": jnp.repeat(gmlp_b_s[l].T, GMLP_WIDTH // GMLP_GROUPS, axis=1)}
        rprm = {"mu": _row(rwkv_mu[l]), "w0": _row(rwkv_w0[l]), "a0": _row(rwkv_a0[l]),
                "w_up": _pad_rows(rwkv_w_up[l], LANES, 0).astype(BF16),
                "a_up": _pad_rows(rwkv_a_up[l], LANES, RWKV_DECAY_LORA).astype(BF16),
                "k_k": _row(rwkv_k_k[l]), "k_a": _row(rwkv_k_a[l]), "r_k": _row(rwkv_r_k[l]),
                "ln_w": _row(rwkv_ln_w[l]), "ln_b": _row(rwkv_ln_b[l]), "hsum": consts["hsum"], "cum": consts["cum"]}
        if l > 0:
            rprm["v0"] = _row(rwkv_v0[l - 1])
            rprm["v_up"] = _pad_rows(rwkv_v_up[l - 1], MISC, 0).astype(BF16)
        prepared, y_b, o_mla, o_mg = _in_proj(x2, _row(pre_norm[l]), w_all, v_first, rprm, gprm, seq, tc)
        if l == 0:
            v_first = prepared[-1]
        y_a = _rwkv(prepared, rprm, bsz, seq, tc)

        mprm = {"spread": spread, "q_norm": _row(mla_q_norm[l]), "w_uq": wq,
                "kv_norm": _row(mla_kv_norm[l]), "w_ukv": wkv}
        q, k, vt = _mla_prep(o_mla, rope_tab, mprm, bsz, seq, tq)
        y_c = _attn(q, k, vt, o_mla, bsz, seq, tq, ATTN_HEADS_PER_STEP)

        x2 = _merge(y_a, y_b, y_c, o_mg, x2, w_branch[l].astype(BF16), w_out[l].astype(BF16),
                    _row(post_norm[l]), tm)
    return x2.reshape(bsz, seq, D_MODEL)
```

```python
import functools

import numpy as np
import jax
import jax.numpy as jnp
from jax import lax
from jax.experimental import pallas as pl
from jax.experimental.pallas import tpu as pltpu

F32 = jnp.float32
BF16 = jnp.bfloat16

D_MODEL = 1024
BRANCH_WIDTH = 512
N_BRANCHES = 3
NORM_EPS = 1e-6
RWKV_HEADS = 8
RWKV_HEAD_DIM = 64
RWKV_WIDTH = RWKV_HEADS * RWKV_HEAD_DIM
RWKV_DECAY_LORA = 64
RWKV_AAA_LORA = 64
RWKV_VRES_LORA = 32
RWKV_GN_EPS = 64e-5
RWKV_SHIFT_COLS = 3 * RWKV_WIDTH + RWKV_DECAY_LORA + RWKV_AAA_LORA
GMLP_WIDTH = 512
GMLP_GROUPS = 8
GMLP_CHUNK = 128
GMLP_LN_EPS = 1e-5
MLA_HEADS = 8
MLA_Q_LORA = 384
MLA_KV_LORA = 256
MLA_NOPE = 64
MLA_ROPE = 32
MLA_V = 64
MLA_WIDTH = MLA_HEADS * MLA_V
ROPE_THETA = 10000.0

LANES = 128
MISC = LANES
RW_COLS = RWKV_SHIFT_COLS + RWKV_WIDTH + MISC
G_COLS = 3 * GMLP_WIDTH
MLA_COLS = MLA_WIDTH + MLA_Q_LORA + MLA_KV_LORA + MISC
MG_COLS = N_BRANCHES * D_MODEL
ALL_COLS = RW_COLS + G_COLS + MLA_COLS + MG_COLS
HEAD_PAD = LANES
ROPE_LANE0 = MLA_NOPE
SUB = 64
PAIR = 2 * SUB

VMEM_LIMIT = 56 * 1024 * 1024
TIME_TILE = 256
ROW_TILE = 256
ATTN_TILE = 512


def _dot(a, b):
    return jnp.dot(a, b, preferred_element_type=F32)


def _dot_nt(a, b):
    return lax.dot_general(a, b, (((1,), (1,)), ((), ())), preferred_element_type=F32)


def _bdot(a, b):
    return lax.dot_general(a, b, (((2,), (1,)), ((0,), (0,))), preferred_element_type=F32)


def _bdot_nt(a, b):
    return lax.dot_general(a, b, (((2,), (2,)), ((0,), (0,))), preferred_element_type=F32)


def _bdot_tn(a, b):
    return lax.dot_general(a, b, (((1,), (1,)), ((0,), (0,))), preferred_element_type=F32)


def _split3(x):
    hi = x.astype(BF16)
    r1 = x - hi.astype(F32)
    mid = r1.astype(BF16)
    lo = (r1 - mid.astype(F32)).astype(BF16)
    return hi, mid, lo


def _dot_split2_lhs(m, x):
    hi = x.astype(BF16)
    lo = (x - hi.astype(F32)).astype(BF16)
    return _dot(m, hi) + _dot(m, lo)


def _dot_exact_rhs(x, m):
    hi, mid, lo = _split3(x)
    return _dot(hi, m) + _dot(mid, m) + _dot(lo, m)


def _head_sum(x, m):
    x16 = x.astype(BF16)
    return jnp.concatenate([_dot(x16[:, c:c + LANES], m) for c in range(0, x.shape[1], LANES)], axis=1)


def _sigmoid(x):
    return 0.5 * jnp.tanh(0.5 * x) + 0.5


def _const_spec(shape):
    nd = len(shape)
    return pl.BlockSpec(shape, lambda *_: (0,) * nd)


RW_PREP_OUTS = 7
PROJ_CHUNK = 512
PC_ROWS = 8
RW_PARAMS = ("mu", "w0", "w_up", "a0", "a_up", "k_k", "k_a", "r_k", "hsum", "cum")
RW_VRES_PARAMS = ("v0", "v_up")


def _in_proj_kernel(has_vres, tm, tiles_per_row, *refs):
    it = iter(refs)
    x_ref, g_ref, w_ref = next(it), next(it), next(it)
    vf_ref = next(it) if has_vres else None
    prm = {nm: next(it) for nm in RW_PARAMS + (RW_VRES_PARAMS if has_vres else ())}
    lnw_ref, lnb_ref, ws_ref, bias_ref = (next(it) for _ in range(4))
    at_o, rt_o, bt_o, kt_o, v_o, bd_o, kd_o, pc_o, bonus_o, gate_o = (next(it) for _ in range(RW_PREP_OUTS + 3))
    vout_o = None if has_vres else next(it)
    o_yb, o_mla, o_mg = next(it), next(it), next(it)
    carry_ref = next(it)

    x = x_ref[...]
    ms = jnp.mean(x * x, axis=-1, keepdims=True)
    h = (x * lax.rsqrt(ms + NORM_EPS) * g_ref[...]).astype(BF16)

    def columns(c0, c1):
        return _dot(h, w_ref[:, c0:c1])

    def project(o, col, c_from, c_to):
        for c0 in range(c_from, c_to, PROJ_CHUNK):
            cw = min(PROJ_CHUNK, c_to - c0)
            o[:, c0:c0 + cw] = columns(col + c0, col + c0 + cw).astype(o.dtype)

    p = jnp.concatenate([columns(c0, min(c0 + PROJ_CHUNK, RWKV_SHIFT_COLS))
                         for c0 in range(0, RWKV_SHIFT_COLS, PROJ_CHUNK)], axis=1)
    rw_gate = columns(RWKV_SHIFT_COLS, RWKV_SHIFT_COLS + RWKV_WIDTH)
    vd = columns(RWKV_SHIFT_COLS + RWKV_WIDTH, RW_COLS) if has_vres else None

    first_tile = (pl.program_id(0) % tiles_per_row) == 0
    row = lax.broadcasted_iota(jnp.int32, p.shape, 0)
    before = jnp.where(first_tile, 0.0, carry_ref[0:1, :])
    shifted = jnp.where(row == 0, before, pltpu.roll(p, 1, axis=0))
    carry_ref[0:1, :] = p[tm - 1:tm, :]
    p = p + (shifted - p) * prm["mu"][...]
    r = p[:, 0:RWKV_WIDTH]
    k = p[:, RWKV_WIDTH:2 * RWKV_WIDTH]
    v = p[:, 2 * RWKV_WIDTH:3 * RWKV_WIDTH]
    lora_in = p[:, 3 * RWKV_WIDTH:RWKV_SHIFT_COLS]
    lora_w = jnp.tanh(lora_in).astype(BF16)
    lora_a = lora_in.astype(BF16)
    hsum = prm["hsum"][...]
    kk = k * prm["k_k"][...]
    kk_sq = (kk * kk)

    gu, gv, gg = (columns(RW_COLS + i * GMLP_WIDTH, RW_COLS + (i + 1) * GMLP_WIDTH) for i in range(3))
    z = -(prm["w0"][...] + _dot(lora_w, prm["w_up"][...]))
    a_pre = prm["a0"][...] + _dot(lora_a, prm["a_up"][...])
    vg_pre = prm["v0"][...] + _dot(vd.astype(BF16), prm["v_up"][...]) if has_vres else None
    kk_norm2 = _head_sum(kk_sq, hsum)

    softplus = jnp.maximum(z, 0.0) + jnp.log1p(jnp.exp(-jnp.abs(z)))
    lw = -jnp.exp(-softplus - 0.5)
    a = _sigmoid(a_pre)
    if has_vres:
        v = v + (vf_ref[...] - v) * _sigmoid(vg_pre)
    else:
        vout_o[...] = v
    kk = kk * lax.rsqrt(jnp.maximum(kk_norm2, 1e-24))
    k2 = k * (1.0 + (a - 1.0) * prm["k_a"][...])
    av = -kk
    bv = kk * a
    rkk = r * k2 * prm["r_k"][...]
    gmlp_state = _gmlp_pointwise(gu, gv, gg, lnw_ref, lnb_ref)

    project(o_mla, RW_COLS + G_COLS, 0, MLA_COLS)
    gs = _dot_split2_lhs(prm["cum"][...], lw)
    bonus = _head_sum(rkk, hsum)
    _gmlp_position_mix(tm, gmlp_state, ws_ref, bias_ref, o_yb)

    project(o_mg, RW_COLS + G_COLS + MLA_COLS, 0, MG_COLS // 2)
    g = gs[0:tm]
    gl = gs[tm:2 * tm]
    e_neg = jnp.exp(-g)
    e_end = jnp.exp(gl - g)
    at_o[...] = (av * jnp.exp(g - lw)).astype(BF16)
    rt_o[...] = (r * jnp.exp(g)).astype(BF16)
    bt_o[...] = (bv * e_neg).astype(BF16)
    kt_o[...] = (k2 * e_neg).astype(BF16)
    v_o[...] = v.astype(BF16)
    bd_o[...] = (bv * e_end).astype(BF16)
    kd_o[...] = (k2 * e_end).astype(BF16)
    pc = jnp.exp(gl)
    nsub = tm // SUB
    pc_o[...] = jnp.concatenate([pc[SUB * c:SUB * c + 1] for c in range(nsub)]
                                + [jnp.zeros((PC_ROWS - nsub, RWKV_WIDTH), F32)], axis=0)
    bonus_o[...] = bonus * v
    gate_o[...] = (rw_gate * _sigmoid(rw_gate)).astype(BF16)
    project(o_mg, RW_COLS + G_COLS + MLA_COLS, MG_COLS // 2, MG_COLS)


def _in_proj(x2, gain, w_all, v_first, rprm, gprm, seq, tm):
    has_vres = v_first is not None
    n = x2.shape[0]
    rowspec = lambda c: pl.BlockSpec((tm, c), lambda i: (i, 0))
    args = [x2, gain, w_all]
    specs = [rowspec(D_MODEL), _const_spec((1, D_MODEL)),
             pl.BlockSpec((D_MODEL, ALL_COLS), lambda i: (0, 0), pipeline_mode=pl.Buffered(1))]
    if has_vres:
        args.append(v_first)
        specs.append(rowspec(RWKV_WIDTH))
    for nm in RW_PARAMS + (RW_VRES_PARAMS if has_vres else ()):
        args.append(rprm[nm])
        specs.append(_const_spec(rprm[nm].shape))
    for nm in ("ln_w", "ln_b", "w_s", "bias"):
        args.append(gprm[nm])
        specs.append(_const_spec(gprm[nm].shape))
    wide = lambda dt: (rowspec(RWKV_WIDTH), jax.ShapeDtypeStruct((n, RWKV_WIDTH), dt))
    outs = [wide(BF16) for _ in range(RW_PREP_OUTS)]
    outs.append((pl.BlockSpec((PC_ROWS, RWKV_WIDTH), lambda i: (i, 0)),
                 jax.ShapeDtypeStruct((n // tm * PC_ROWS, RWKV_WIDTH), F32)))
    outs += [wide(F32), wide(BF16)]
    if not has_vres:
        outs.append(wide(F32))
    outs += [(rowspec(c), jax.ShapeDtypeStruct((n, c), BF16)) for c in (GMLP_WIDTH, MLA_COLS, MG_COLS)]
    res = pl.pallas_call(
        functools.partial(_in_proj_kernel, has_vres, tm, seq // tm),
        grid=(n // tm,),
        in_specs=specs,
        out_specs=[o[0] for o in outs],
        out_shape=[o[1] for o in outs],
        scratch_shapes=[pltpu.VMEM((8, RWKV_SHIFT_COLS), F32)],
        compiler_params=pltpu.CompilerParams(dimension_semantics=("arbitrary",), vmem_limit_bytes=VMEM_LIMIT),
        name="in_proj_vres" if has_vres else "in_proj",
    )(*args)
    nrw = RW_PREP_OUTS + 3 + (0 if has_vres else 1)
    return res[:nrw], res[nrw], res[nrw + 1], res[nrw + 2]


def _rwkv_kernel(tc, *refs):
    it = iter(refs)
    prepared = [next(it) for _ in range(RW_PREP_OUTS)]
    pc_ref, bonus_ref, gate_ref, lnw_ref, lnb_ref, hsum_ref = (next(it) for _ in range(6))
    y_ref = next(it)
    state_ref = next(it)
    stacked = [next(it) for _ in range(RW_PREP_OUTS)]
    pc_s, y_s = next(it), next(it)

    @pl.when(pl.program_id(1) == 0)
    def _():
        state_ref[...] = jnp.zeros_like(state_ref)

    nsub = tc // SUB
    npair = RWKV_HEADS // 2

    def pieces(x):
        for c in range(nsub):
            for j in range(npair):
                yield c * npair + j, x[SUB * c:SUB * (c + 1), LANES * j:LANES * (j + 1)]

    lane = lax.broadcasted_iota(jnp.int32, (tc, RWKV_WIDTH), 1)
    first16 = jnp.where((lane % LANES) < RWKV_HEAD_DIM, 1.0, 0.0).astype(BF16)
    for src, dst in zip(prepared, stacked):
        x16 = src[...]
        xa = x16 * first16
        xb = x16 - xa
        for b, piece in pieces(xa):
            dst[b, 0:SUB, :] = piece
        for b, piece in pieces(xb):
            dst[b, SUB:PAIR, :] = piece
    pc = pc_ref[...]
    for c in range(nsub):
        for j in range(npair):
            pc_s[c * npair + j] = jnp.broadcast_to(pc[c:c + 1, LANES * j:LANES * (j + 1)], (PAIR, LANES))

    ri = lax.broadcasted_iota(jnp.int32, (PAIR, PAIR), 0)
    ci = lax.broadcasted_iota(jnp.int32, (PAIR, PAIR), 1)
    strict = ri > ci
    incl = ri >= ci
    eye = jnp.where(ri == ci, 1.0, 0.0).astype(F32)
    levels = (4, 8, 16, 32, 64)
    lvl_mask = [strict & ((ri // b) == (ci // b)) & ((ri // (b // 2)) != (ci // (b // 2))) for b in levels]
    lvl2 = strict & ((ri // 2) == (ci // 2))

    at, rt, bt, kt, vv, bd, kd = (s[...] for s in stacked)
    s = _bdot_nt(jnp.concatenate([at, rt], axis=1), jnp.concatenate([bt, kt], axis=1))
    nab = jnp.where(strict, s[:, 0:PAIR, 0:PAIR], 0.0)
    aak = jnp.where(strict, s[:, 0:PAIR, PAIR:2 * PAIR], 0.0).astype(BF16)
    rb = jnp.where(incl, s[:, PAIR:2 * PAIR, 0:PAIR], 0.0).astype(BF16)
    rk = jnp.where(incl, s[:, PAIR:2 * PAIR, PAIR:2 * PAIR], 0.0).astype(BF16)
    d = eye + jnp.where(lvl2, nab, 0.0)
    for m in lvl_mask:
        e = jnp.where(m, nab, 0.0).astype(BF16)
        db = d.astype(BF16)
        d = d + _bdot(_bdot(db, e).astype(BF16), db)
    t = d.astype(BF16)
    w1 = _bdot(aak, vv).astype(BF16)
    au = _bdot(t, jnp.concatenate([at, w1], axis=2))
    ah = au[:, :, 0:PAIR].astype(BF16)
    uh = au[:, :, PAIR:2 * PAIR]
    kv = _bdot_tn(kd, vv)
    pcm = jnp.swapaxes(pc_s[...], 1, 2)
    rbk = jnp.concatenate([rb, rk], axis=2)

    h = state_ref[...]
    for c in range(nsub):
        sl = slice(c * npair, (c + 1) * npair)
        arh = _bdot(jnp.concatenate([ah[sl], rt[sl]], axis=1), h.astype(BF16))
        u = (arh[:, 0:PAIR] + uh[sl]).astype(BF16)
        h = pcm[sl] * h + _bdot_tn(bd[sl], u) + kv[sl]
        y = arh[:, PAIR:2 * PAIR] + _bdot(rbk[sl], jnp.concatenate([u, vv[sl]], axis=1))
        for j in range(npair):
            y_s[SUB * c:SUB * (c + 1), LANES * j:LANES * (j + 1)] = y[j, 0:SUB] + y[j, SUB:PAIR]
    state_ref[...] = h

    y = y_s[...]
    hsum = hsum_ref[...]
    inv_n = 1.0 / RWKV_HEAD_DIM
    mean = _head_sum(y, hsum) * inv_n
    yc = y - mean
    var = _head_sum(yc * yc, hsum) * inv_n
    y = yc * lax.rsqrt(var + RWKV_GN_EPS) * lnw_ref[...] + lnb_ref[...]
    y_ref[...] = ((y + bonus_ref[...]) * gate_ref[...].astype(F32)).astype(y_ref.dtype)


def _rwkv(prepared, prm, bsz, seq, tc):
    n = bsz * seq
    nt = seq // tc
    row = lambda b, t: (b * nt + t, 0)
    wide = pl.BlockSpec((tc, RWKV_WIDTH), row)
    specs = [wide] * RW_PREP_OUTS + [pl.BlockSpec((PC_ROWS, RWKV_WIDTH), row), wide, wide]
    args = list(prepared[:RW_PREP_OUTS + 3])
    for nm in ("ln_w", "ln_b", "hsum"):
        args.append(prm[nm])
        specs.append(_const_spec(prm[nm].shape))
    nb = (tc // SUB) * (RWKV_HEADS // 2)
    scratch = [pltpu.VMEM((RWKV_HEADS // 2, PAIR, PAIR), F32)]
    scratch += [pltpu.VMEM((nb, PAIR, PAIR), BF16) for _ in range(RW_PREP_OUTS)]
    scratch += [pltpu.VMEM((nb, PAIR, PAIR), F32), pltpu.VMEM((tc, RWKV_WIDTH), F32)]
    return pl.pallas_call(
        functools.partial(_rwkv_kernel, tc),
        grid=(bsz, nt),
        in_specs=specs,
        out_specs=wide,
        out_shape=jax.ShapeDtypeStruct((n, RWKV_WIDTH), BF16),
        scratch_shapes=scratch,
        compiler_params=pltpu.CompilerParams(dimension_semantics=("arbitrary", "arbitrary"),
                                             vmem_limit_bytes=VMEM_LIMIT),
        name="rwkv",
    )(*args)


def _gelu(x):
    return 0.5 * x * (1.0 + lax.erf(x * np.float32(1.0 / np.sqrt(2.0))))


def _gmlp_pointwise(u, v, gate, lnw_ref, lnb_ref):
    u = _gelu(u)
    v = _gelu(v)
    mu = jnp.mean(v, axis=-1, keepdims=True)
    vc = v - mu
    var = jnp.mean(vc * vc, axis=-1, keepdims=True)
    v = vc * lax.rsqrt(var + GMLP_LN_EPS) * lnw_ref[...] + lnb_ref[...]
    return u * gate * _sigmoid(gate), v


def _gmlp_position_mix(tm, state, ws_ref, bias_ref, o_ref):
    ug, v = state
    lane = lax.broadcasted_iota(jnp.int32, (GMLP_CHUNK, LANES), 1)
    first = lane < (LANES // 2)
    ri = lax.broadcasted_iota(jnp.int32, (GMLP_CHUNK, 2 * GMLP_CHUNK), 0)
    ci = lax.broadcasted_iota(jnp.int32, (GMLP_CHUNK, 2 * GMLP_CHUNK), 1)
    causal = ri >= (ci % GMLP_CHUNK)
    for j in range(GMLP_GROUPS // 2):
        w = jnp.where(causal, ws_ref[j], 0.0).astype(BF16)
        ls = slice(LANES * j, LANES * (j + 1))
        for c in range(tm // GMLP_CHUNK):
            rs = slice(GMLP_CHUNK * c, GMLP_CHUNK * (c + 1))
            vp = v[rs, ls]
            stacked = jnp.concatenate([jnp.where(first, vp, 0.0), jnp.where(first, 0.0, vp)], axis=0).astype(BF16)
            vm = _dot(w, stacked) + bias_ref[:, ls]
            o_ref[rs, ls] = (ug[rs, ls] * vm).astype(o_ref.dtype)


def _rms(x, gain):
    return x * lax.rsqrt(jnp.mean(x * x, axis=-1, keepdims=True) + NORM_EPS) * gain


def _rope_table_kernel(pos_ref, invf_ref, o_ref):
    ang = pos_ref[...].astype(F32) * invf_ref[...]
    lane = lax.broadcasted_iota(jnp.int32, ang.shape, 1)
    o_ref[...] = jnp.where((lane % MLA_ROPE) < MLA_ROPE // 2, jnp.cos(ang), jnp.sin(ang))


def _rope_table(positions, inv_freq):
    n = positions.size
    per_row = LANES // MLA_ROPE
    pos = jnp.repeat(positions.reshape(n // per_row, per_row), MLA_ROPE, axis=1)
    invf = jnp.tile(inv_freq, LANES // inv_freq.size).reshape(1, LANES)
    rows = n // per_row
    tr = min(1024, rows)
    tab = pl.pallas_call(
        _rope_table_kernel,
        grid=(rows // tr,),
        in_specs=[pl.BlockSpec((tr, LANES), lambda i: (i, 0)), _const_spec((1, LANES))],
        out_specs=pl.BlockSpec((tr, LANES), lambda i: (i, 0)),
        out_shape=jax.ShapeDtypeStruct((rows, LANES), F32),
        name="rope_table",
    )(pos, invf)
    return tab.reshape(n, MLA_ROPE)


def _rope_spread():
    half = MLA_ROPE // 2
    m = np.zeros((MLA_ROPE, 2 * LANES), np.float32)
    for i in range(half):
        m[i, ROPE_LANE0 + i] = 1.0
        m[i, ROPE_LANE0 + half + i] = 1.0
        m[half + i, LANES + ROPE_LANE0 + i] = -1.0
        m[half + i, LANES + ROPE_LANE0 + half + i] = 1.0
    return jnp.asarray(m, BF16)


def _with_swapped_rope(w):
    half = MLA_ROPE // 2
    lo, hi = ROPE_LANE0, ROPE_LANE0 + MLA_ROPE
    return jnp.concatenate([w[..., :hi], w[..., lo + half:hi], w[..., lo:lo + half]], axis=-1)


def _mla_prep_kernel(m_ref, cs_ref, spread_ref, qn_ref, wq_ref, kvn_ref, wkv_ref, q_ref, k_ref, vt_ref):
    c0 = MLA_WIDTH
    c_q = m_ref[:, c0:c0 + MLA_Q_LORA].astype(F32)
    c_kv = m_ref[:, c0 + MLA_Q_LORA:c0 + MLA_Q_LORA + MLA_KV_LORA].astype(F32)
    k_rope = m_ref[:, c0 + MLA_Q_LORA + MLA_KV_LORA:MLA_COLS].astype(F32)
    tab = _dot_exact_rhs(cs_ref[...], spread_ref[...])
    lane = lax.broadcasted_iota(jnp.int32, (1, LANES), 1)
    cos = tab[:, 0:LANES] + jnp.where(lane < ROPE_LANE0, 1.0, 0.0)
    sin = tab[:, LANES:2 * LANES]

    def rope(x):
        return x * cos + pltpu.roll(x, LANES - MLA_ROPE, axis=1) * sin

    scale = np.float32((MLA_NOPE + MLA_ROPE) ** -0.5 * np.log2(np.e))
    q = _dot(_rms(c_q, qn_ref[...]).astype(BF16), wq_ref[...])
    kv = _dot(_rms(c_kv, kvn_ref[...]).astype(BF16), wkv_ref[...])
    kr = rope(k_rope)
    for h in range(MLA_HEADS):
        ls = slice(HEAD_PAD * h, HEAD_PAD * (h + 1))
        q_ref[:, ls] = (rope(q[:, ls]) * scale).astype(BF16)
        k_ref[:, ls] = (kv[:, ls] + kr).astype(BF16)
    vl = lax.broadcasted_iota(jnp.int32, (1, HEAD_PAD), 1)
    for h in range(MLA_HEADS):
        vh = kv[:, (MLA_HEADS + h) * HEAD_PAD:(MLA_HEADS + h + 1) * HEAD_PAD]
        vt_ref[0, h, 0] = jnp.where(vl == MLA_V, 1.0, vh).T.astype(BF16)


def _mla_prep(o_mla, rope_tab, prm, bsz, seq, tm):
    n = o_mla.shape[0]
    nt = seq // tm
    wide = MLA_HEADS * HEAD_PAD
    return pl.pallas_call(
        _mla_prep_kernel,
        grid=(n // tm,),
        in_specs=[pl.BlockSpec((tm, MLA_COLS), lambda i: (i, 0)),
                  pl.BlockSpec((tm, MLA_ROPE), lambda i: (i, 0)),
                  _const_spec(prm["spread"].shape),
                  _const_spec((1, MLA_Q_LORA)), _const_spec(prm["w_uq"].shape),
                  _const_spec((1, MLA_KV_LORA)), _const_spec(prm["w_ukv"].shape)],
        out_specs=[pl.BlockSpec((tm, wide), lambda i: (i, 0)), pl.BlockSpec((tm, wide), lambda i: (i, 0)),
                   pl.BlockSpec((1, MLA_HEADS, 1, HEAD_PAD, tm), lambda i: (i // nt, 0, i % nt, 0, 0))],
        out_shape=[jax.ShapeDtypeStruct((n, wide), BF16), jax.ShapeDtypeStruct((n, wide), BF16),
                   jax.ShapeDtypeStruct((bsz, MLA_HEADS, nt, HEAD_PAD, tm), BF16)],
        compiler_params=pltpu.CompilerParams(dimension_semantics=("arbitrary",), vmem_limit_bytes=VMEM_LIMIT),
        name="mla_prep",
    )(o_mla, rope_tab, prm["spread"], prm["q_norm"], prm["w_uq"], prm["kv_norm"], prm["w_ukv"])


ATTN_HEADS_PER_STEP = 4
V_ROWS = 80


def _attn_kernel(tq, nh, q_ref, k_ref, vt_ref, gate_ref, o_ref, m_s, acc_s):
    i = pl.program_id(2)
    ki = lax.broadcasted_iota(jnp.int32, (tq, tq), 0)
    qi = lax.broadcasted_iota(jnp.int32, (tq, tq), 1)
    causal = ki <= qi
    heads = [slice(HEAD_PAD * hh, HEAD_PAD * (hh + 1)) for hh in range(nh)]
    m_s[...] = jnp.full(m_s.shape, -jnp.inf, F32)
    acc_s[...] = jnp.zeros(acc_s.shape, F32)
    groups = tq // 8

    def block(j, masked):
        r0 = pl.multiple_of(j * tq, tq)
        sts = [_dot_nt(k_ref[pl.ds(r0, tq), ls], q_ref[:, ls]) for ls in heads]
        for hh, st in enumerate(sts):
            if masked:
                st = jnp.where(causal, st, -jnp.inf)
            s3 = st.reshape(groups, 8, tq)
            mc = jnp.max(s3, axis=0)
            for sh in (4, 2, 1):
                mc = jnp.maximum(mc, pltpu.roll(mc, sh, axis=0))
            m_old = m_s[hh]
            m_new = jnp.maximum(m_old, mc)
            p = jnp.exp2(s3 - m_new[None]).reshape(tq, tq).astype(BF16)
            alpha = jnp.exp2(m_old - m_new)
            pv = _dot(vt_ref[0, hh, j, 0:V_ROWS, :], p)
            acc_s[hh] = (acc_s[hh].reshape(V_ROWS // 8, 8, tq) * alpha[None]).reshape(V_ROWS, tq) + pv
            m_s[hh] = m_new

    def full_block(j, c):
        block(j, False)
        return c

    lax.fori_loop(0, i, full_block, 0)
    block(i, True)
    outs = []
    for hh in range(nh):
        acc = acc_s[hh]
        outs.append(acc[0:MLA_V] * (1.0 / acc[MLA_V:MLA_V + 1]))
    out = jnp.concatenate(outs, axis=0).T
    gate = gate_ref[...].astype(F32)
    o_ref[...] = (out * gate * _sigmoid(gate)).astype(o_ref.dtype)


def _attn(q, k, vt, o_mla, bsz, seq, tq, nh):
    n = bsz * seq
    nq = seq // tq
    return pl.pallas_call(
        functools.partial(_attn_kernel, tq, nh),
        grid=(bsz, MLA_HEADS // nh, nq),
        in_specs=[pl.BlockSpec((tq, nh * HEAD_PAD), lambda b, h, i: (b * nq + i, h)),
                  pl.BlockSpec((seq, nh * HEAD_PAD), lambda b, h, i: (b, h)),
                  pl.BlockSpec((1, nh, nq, HEAD_PAD, tq), lambda b, h, i: (b, h, 0, 0, 0)),
                  pl.BlockSpec((tq, nh * MLA_V), lambda b, h, i: (b * nq + i, h))],
        out_specs=pl.BlockSpec((tq, nh * MLA_V), lambda b, h, i: (b * nq + i, h)),
        out_shape=jax.ShapeDtypeStruct((n, MLA_WIDTH), BF16),
        scratch_shapes=[pltpu.VMEM((nh, 8, tq), F32), pltpu.VMEM((nh, V_ROWS, tq), F32)],
        compiler_params=pltpu.CompilerParams(dimension_semantics=("arbitrary",) * 3, vmem_limit_bytes=VMEM_LIMIT),
        name="attn",
    )(q, k, vt, o_mla)


def _merge_kernel(ya_ref, yb_ref, yc_ref, mg_ref, x_ref, wb_ref, wo_ref, pn_ref, o_ref):
    ys = (ya_ref[...], yb_ref[...], yc_ref[...])
    halves = []
    for c0 in range(0, D_MODEL, PROJ_CHUNK):
        acc = None
        for nb in range(N_BRANCHES):
            gates = _sigmoid(mg_ref[:, nb * D_MODEL + c0:nb * D_MODEL + c0 + PROJ_CHUNK].astype(F32))
            term = gates * _dot(ys[nb], wb_ref[nb, :, c0:c0 + PROJ_CHUNK])
            acc = term if acc is None else acc + term
        halves.append(acc.astype(BF16))
    mixed = _dot(jnp.concatenate(halves, axis=1), wo_ref[...])
    ms = jnp.mean(mixed * mixed, axis=-1, keepdims=True)
    o_ref[...] = x_ref[...] + mixed * lax.rsqrt(ms + NORM_EPS) * pn_ref[...]


def _merge(ya, yb, yc, o_mg, x2, w_branch, w_out, post_norm, tm):
    n = x2.shape[0]
    rowspec = lambda c: pl.BlockSpec((tm, c), lambda i: (i, 0))
    return pl.pallas_call(
        _merge_kernel,
        grid=(n // tm,),
        in_specs=[rowspec(BRANCH_WIDTH), rowspec(BRANCH_WIDTH), rowspec(BRANCH_WIDTH), rowspec(MG_COLS),
                  rowspec(D_MODEL), _const_spec(w_branch.shape), _const_spec(w_out.shape), _const_spec((1, D_MODEL))],
        out_specs=rowspec(D_MODEL),
        out_shape=jax.ShapeDtypeStruct((n, D_MODEL), F32),
        compiler_params=pltpu.CompilerParams(dimension_semantics=("arbitrary",), vmem_limit_bytes=VMEM_LIMIT),
        name="merge",
    )(ya, yb, yc, o_mg, x2, w_branch, w_out, post_norm)


def _row(v):
    return v.reshape(1, -1).astype(F32)


def _pad_rows(w, rows, at):
    out = jnp.zeros((rows, w.shape[1]), w.dtype)
    return out.at[at:at + w.shape[0]].set(w)


IN_SEGS = (RWKV_SHIFT_COLS, RWKV_WIDTH, GMLP_WIDTH, GMLP_WIDTH, GMLP_WIDTH,
           MLA_Q_LORA, MLA_KV_LORA, MLA_ROPE, MLA_WIDTH, N_BRANCHES * D_MODEL)
W_ROWS = 128


def _w_layout_kernel(has_vres, *refs):
    w_ref = refs[0]
    vd_ref = refs[1] if has_vres else None
    o_ref = refs[-1]
    o = [int(c) for c in np.cumsum((0,) + IN_SEGS)]
    cols = lambda c0, c1: w_ref[0, :, c0:c1]
    seg = lambda i: cols(o[i], o[i + 1])
    rows = w_ref.shape[1]
    zeros = lambda c: jnp.zeros((rows, c), F32)
    half = MLA_ROPE // 2
    k_rope = seg(7)
    misc_a = jnp.concatenate([vd_ref[0], zeros(MISC - RWKV_VRES_LORA)], axis=1) if has_vres else zeros(MISC)
    misc_b = jnp.concatenate([zeros(ROPE_LANE0), k_rope, k_rope[:, half:], k_rope[:, :half]], axis=1)
    col = 0
    for piece in (cols(o[0], o[2]), misc_a, cols(o[2], o[5]), seg(8), seg(5), seg(6), misc_b, seg(9)):
        o_ref[:, col:col + piece.shape[1]] = piece.astype(BF16)
        col += piece.shape[1]


def _w_layout(l, w_in, v_down):
    has_vres = l > 0
    args = [w_in] + ([v_down] if has_vres else [])
    specs = [pl.BlockSpec((1, W_ROWS, w_in.shape[2]), lambda i: (l, i, 0))]
    if has_vres:
        specs.append(pl.BlockSpec((1, W_ROWS, RWKV_VRES_LORA), lambda i: (l - 1, i, 0)))
    return pl.pallas_call(
        functools.partial(_w_layout_kernel, has_vres),
        grid=(D_MODEL // W_ROWS,),
        in_specs=specs,
        out_specs=pl.BlockSpec((W_ROWS, ALL_COLS), lambda i: (i, 0)),
        out_shape=jax.ShapeDtypeStruct((D_MODEL, ALL_COLS), BF16),
        compiler_params=pltpu.CompilerParams(dimension_semantics=("arbitrary",), vmem_limit_bytes=VMEM_LIMIT),
        name="w_layout",
    )(*args)


def _layer_weights(l, w_in, rwkv_v_down, mla_w_uq, mla_w_ukv):
    w_all = _w_layout(l, w_in, rwkv_v_down)
    qd = MLA_NOPE + MLA_ROPE
    wq = mla_w_uq[l].reshape(MLA_Q_LORA, MLA_HEADS, qd)
    wq = _with_swapped_rope(jnp.pad(wq, ((0, 0), (0, 0), (0, HEAD_PAD - qd)))).reshape(MLA_Q_LORA, MLA_HEADS * HEAD_PAD)
    wkv = mla_w_ukv[l].reshape(MLA_KV_LORA, MLA_HEADS, MLA_NOPE + MLA_V)
    wk = jnp.pad(wkv[:, :, :MLA_NOPE], ((0, 0), (0, 0), (0, HEAD_PAD - MLA_NOPE)))
    wv = jnp.pad(wkv[:, :, MLA_NOPE:], ((0, 0), (0, 0), (0, HEAD_PAD - MLA_V)))
    wkv_all = jnp.concatenate([wk.reshape(MLA_KV_LORA, -1), wv.reshape(MLA_KV_LORA, -1)], axis=1)
    return w_all, wq.astype(BF16), wkv_all.astype(BF16)


def _constants(tc):
    head = np.arange(LANES) // RWKV_HEAD_DIM
    hsum = (head[:, None] == head[None, :]).astype(np.float32)
    t = np.arange(tc)
    same = (t[:, None] // SUB) == (t[None, :] // SUB)
    cum = np.concatenate([same & (t[:, None] >= t[None, :]), same], axis=0).astype(np.float32)
    return {"hsum": jnp.asarray(hsum, BF16), "cum": jnp.asarray(cum, BF16)}


def kernel(x, positions, w_in, pre_norm, post_norm, rwkv_mu, rwkv_w0, rwkv_w_up, rwkv_a0, rwkv_a_up, rwkv_k_k, rwkv_k_a, rwkv_r_k, rwkv_ln_w, rwkv_ln_b, rwkv_v0, rwkv_v_down, rwkv_v_up, gmlp_ln_w, gmlp_ln_b, gmlp_w_s, gmlp_b_s, mla_q_norm, mla_w_uq, mla_kv_norm, mla_w_ukv, w_branch, w_out):
    bsz, seq, _ = x.shape
    n = bsz * seq
    depth = w_in.shape[0]
    tc, tm, tq = min(TIME_TILE, seq), min(ROW_TILE, n), min(ATTN_TILE, seq)
    consts = _constants(tc)
    inv_freq = 1.0 / (ROPE_THETA ** (jnp.arange(0, MLA_ROPE, 2, dtype=F32) / MLA_ROPE))
    rope_tab = _rope_table(positions, inv_freq)
    spread = _rope_spread()

    x2 = x.reshape(n, D_MODEL)
    v_first = None
    for l in range(depth):
        w_all, wq, wkv = _layer_weights(l, w_in, rwkv_v_down, mla_w_uq, mla_w_ukv)
        ws = gmlp_w_s[l].reshape(GMLP_GROUPS // 2, 2, GMLP_CHUNK, GMLP_CHUNK)
        gprm = {"ln_w": _row(gmlp_ln_w[l]), "ln_b": _row(gmlp_ln_b[l]),
                "w_s": jnp.concatenate([ws[:, 0], ws[:, 1]], axis=-1),
                "bias": jnp.repeat(gmlp_b_s[l].T, GMLP_WIDTH // GMLP_GROUPS, axis=1)}
        rprm = {"mu": _row(rwkv_mu[l]), "w0": _row(rwkv_w0[l]), "a0": _row(rwkv_a0[l]),
                "w_up": _pad_rows(rwkv_w_up[l], LANES, 0).astype(BF16),
                "a_up": _pad_rows(rwkv_a_up[l], LANES, RWKV_DECAY_LORA).astype(BF16),
                "k_k": _row(rwkv_k_k[l]), "k_a": _row(rwkv_k_a[l]), "r_k": _row(rwkv_r_k[l]),
                "ln_w": _row(rwkv_ln_w[l]), "ln_b": _row(rwkv_ln_b[l]), "hsum": consts["hsum"], "cum": consts["cum"]}
        if l > 0:
            rprm["v0"] = _row(rwkv_v0[l - 1])
            rprm["v_up"] = _pad_rows(rwkv_v_up[l - 1], MISC, 0).astype(BF16)
        prepared, y_b, o_mla, o_mg = _in_proj(x2, _row(pre_norm[l]), w_all, v_first, rprm, gprm, seq, tc)
        if l == 0:
            v_first = prepared[-1]
        y_a = _rwkv(prepared, rprm, bsz, seq, tc)

        mprm = {"spread": spread, "q_norm": _row(mla_q_norm[l]), "w_uq": wq,
                "kv_norm": _row(mla_kv_norm[l]), "w_ukv": wkv}
        q, k, vt = _mla_prep(o_mla, rope_tab, mprm, bsz, seq, tq)
        y_c = _attn(q, k, vt, o_mla, bsz, seq, tq, ATTN_HEADS_PER_STEP)

        x2 = _merge(y_a, y_b, y_c, o_mg, x2, w_branch[l].astype(BF16), w_out[l].astype(BF16),
                    _row(post_norm[l]), tm)
    return x2.reshape(bsz, seq, D_MODEL)
```

```python
import functools

import numpy as np
import jax
import jax.numpy as jnp
from jax import lax
from jax.experimental import pallas as pl
from jax.experimental.pallas import tpu as pltpu

F32 = jnp.float32
BF16 = jnp.bfloat16

D_MODEL = 1024
BRANCH_WIDTH = 512
N_BRANCHES = 3
NORM_EPS = 1e-6
RWKV_HEADS = 8
RWKV_HEAD_DIM = 64
RWKV_WIDTH = RWKV_HEADS * RWKV_HEAD_DIM
RWKV_DECAY_LORA = 64
RWKV_AAA_LORA = 64
RWKV_VRES_LORA = 32
RWKV_GN_EPS = 64e-5
RWKV_SHIFT_COLS = 3 * RWKV_WIDTH + RWKV_DECAY_LORA + RWKV_AAA_LORA
GMLP_WIDTH = 512
GMLP_GROUPS = 8
GMLP_CHUNK = 128
GMLP_LN_EPS = 1e-5
MLA_HEADS = 8
MLA_Q_LORA = 384
MLA_KV_LORA = 256
MLA_NOPE = 64
MLA_ROPE = 32
MLA_V = 64
MLA_WIDTH = MLA_HEADS * MLA_V
ROPE_THETA = 10000.0

LANES = 128
MISC = LANES
RW_COLS = RWKV_SHIFT_COLS + RWKV_WIDTH + MISC
G_COLS = 3 * GMLP_WIDTH
MLA_COLS = MLA_WIDTH + MLA_Q_LORA + MLA_KV_LORA + MISC
MG_COLS = N_BRANCHES * D_MODEL
ALL_COLS = RW_COLS + G_COLS + MLA_COLS + MG_COLS
HEAD_PAD = LANES
ROPE_LANE0 = MLA_NOPE
SUB = 64
PAIR = 2 * SUB

VMEM_LIMIT = 56 * 1024 * 1024
TIME_TILE = 256
ROW_TILE = 512
ATTN_TILE = 512


def _dot(a, b):
    return jnp.dot(a, b, preferred_element_type=F32)


def _dot_nt(a, b):
    return lax.dot_general(a, b, (((1,), (1,)), ((), ())), preferred_element_type=F32)


def _bdot(a, b):
    return lax.dot_general(a, b, (((2,), (1,)), ((0,), (0,))), preferred_element_type=F32)


def _bdot_nt(a, b):
    return lax.dot_general(a, b, (((2,), (2,)), ((0,), (0,))), preferred_element_type=F32)


def _bdot_tn(a, b):
    return lax.dot_general(a, b, (((1,), (1,)), ((0,), (0,))), preferred_element_type=F32)


def _split3(x):
    hi = x.astype(BF16)
    r1 = x - hi.astype(F32)
    mid = r1.astype(BF16)
    lo = (r1 - mid.astype(F32)).astype(BF16)
    return hi, mid, lo


def _dot_split2_lhs(m, x):
    hi = x.astype(BF16)
    lo = (x - hi.astype(F32)).astype(BF16)
    return _dot(m, hi) + _dot(m, lo)


def _dot_exact_rhs(x, m):
    hi, mid, lo = _split3(x)
    return _dot(hi, m) + _dot(mid, m) + _dot(lo, m)


def _head_sum(x, m):
    x16 = x.astype(BF16)
    return jnp.concatenate([_dot(x16[:, c:c + LANES], m) for c in range(0, x.shape[1], LANES)], axis=1)


def _sigmoid(x):
    return 0.5 * jnp.tanh(0.5 * x) + 0.5


def _const_spec(shape):
    nd = len(shape)
    return pl.BlockSpec(shape, lambda *_: (0,) * nd)


RW_PREP_OUTS = 7
PROJ_CHUNK = 512
PC_ROWS = 8
RW_PARAMS = ("mu", "w0", "w_up", "a0", "a_up", "k_k", "k_a", "r_k", "hsum", "cum")
RW_VRES_PARAMS = ("v0", "v_up")


def _in_proj_kernel(has_vres, tm, tiles_per_row, *refs):
    it = iter(refs)
    x_ref, g_ref, w_ref = next(it), next(it), next(it)
    vf_ref = next(it) if has_vres else None
    prm = {nm: next(it) for nm in RW_PARAMS + (RW_VRES_PARAMS if has_vres else ())}
    lnw_ref, lnb_ref, ws_ref, bias_ref = (next(it) for _ in range(4))
    at_o, rt_o, bt_o, kt_o, v_o, bd_o, kd_o, pc_o, bonus_o, gate_o = (next(it) for _ in range(RW_PREP_OUTS + 3))
    vout_o = None if has_vres else next(it)
    o_yb, o_mla, o_mg = next(it), next(it), next(it)
    carry_ref = next(it)

    x = x_ref[...]
    ms = jnp.mean(x * x, axis=-1, keepdims=True)
    h = (x * lax.rsqrt(ms + NORM_EPS) * g_ref[...]).astype(BF16)

    def columns(c0, c1):
        return _dot(h, w_ref[:, c0:c1])

    def project(o, col, c_from, c_to):
        for c0 in range(c_from, c_to, PROJ_CHUNK):
            cw = min(PROJ_CHUNK, c_to - c0)
            o[:, c0:c0 + cw] = columns(col + c0, col + c0 + cw).astype(o.dtype)

    p = jnp.concatenate([columns(c0, min(c0 + PROJ_CHUNK, RWKV_SHIFT_COLS))
                         for c0 in range(0, RWKV_SHIFT_COLS, PROJ_CHUNK)], axis=1)
    rw_gate = columns(RWKV_SHIFT_COLS, RWKV_SHIFT_COLS + RWKV_WIDTH)
    vd = columns(RWKV_SHIFT_COLS + RWKV_WIDTH, RW_COLS) if has_vres else None

    first_tile = (pl.program_id(0) % tiles_per_row) == 0
    row = lax.broadcasted_iota(jnp.int32, p.shape, 0)
    before = jnp.where(first_tile, 0.0, carry_ref[0:1, :])
    shifted = jnp.where(row == 0, before, pltpu.roll(p, 1, axis=0))
    carry_ref[0:1, :] = p[tm - 1:tm, :]
    p = p + (shifted - p) * prm["mu"][...]
    r = p[:, 0:RWKV_WIDTH]
    k = p[:, RWKV_WIDTH:2 * RWKV_WIDTH]
    v = p[:, 2 * RWKV_WIDTH:3 * RWKV_WIDTH]
    lora_in = p[:, 3 * RWKV_WIDTH:RWKV_SHIFT_COLS]
    lora_w = jnp.tanh(lora_in).astype(BF16)
    lora_a = lora_in.astype(BF16)
    hsum = prm["hsum"][...]
    kk = k * prm["k_k"][...]
    kk_sq = (kk * kk)

    gu, gv, gg = (columns(RW_COLS + i * GMLP_WIDTH, RW_COLS + (i + 1) * GMLP_WIDTH) for i in range(3))
    z = -(prm["w0"][...] + _dot(lora_w, prm["w_up"][...]))
    a_pre = prm["a0"][...] + _dot(lora_a, prm["a_up"][...])
    vg_pre = prm["v0"][...] + _dot(vd.astype(BF16), prm["v_up"][...]) if has_vres else None
    kk_norm2 = _head_sum(kk_sq, hsum)

    softplus = jnp.maximum(z, 0.0) + jnp.log1p(jnp.exp(-jnp.abs(z)))
    lw = -jnp.exp(-softplus - 0.5)
    a = _sigmoid(a_pre)
    if has_vres:
        v = v + (vf_ref[...] - v) * _sigmoid(vg_pre)
    else:
        vout_o[...] = v
    kk = kk * lax.rsqrt(jnp.maximum(kk_norm2, 1e-24))
    k2 = k * (1.0 + (a - 1.0) * prm["k_a"][...])
    av = -kk
    bv = kk * a
    rkk = r * k2 * prm["r_k"][...]
    gmlp_state = _gmlp_pointwise(gu, gv, gg, lnw_ref, lnb_ref)

    project(o_mla, RW_COLS + G_COLS, 0, MLA_COLS)
    gs = _dot_split2_lhs(prm["cum"][...], lw)
    bonus = _head_sum(rkk, hsum)
    _gmlp_position_mix(tm, gmlp_state, ws_ref, bias_ref, o_yb)

    project(o_mg, RW_COLS + G_COLS + MLA_COLS, 0, MG_COLS // 2)
    g = gs[0:tm]
    gl = gs[tm:2 * tm]
    e_neg = jnp.exp(-g)
    e_end = jnp.exp(gl - g)
    at_o[...] = (av * jnp.exp(g - lw)).astype(BF16)
    rt_o[...] = (r * jnp.exp(g)).astype(BF16)
    bt_o[...] = (bv * e_neg).astype(BF16)
    kt_o[...] = (k2 * e_neg).astype(BF16)
    v_o[...] = v.astype(BF16)
    bd_o[...] = (bv * e_end).astype(BF16)
    kd_o[...] = (k2 * e_end).astype(BF16)
    pc = jnp.exp(gl)
    nsub = tm // SUB
    pc_o[...] = jnp.concatenate([pc[SUB * c:SUB * c + 1] for c in range(nsub)]
                                + [jnp.zeros((PC_ROWS - nsub, RWKV_WIDTH), F32)], axis=0)
    bonus_o[...] = bonus * v
    gate_o[...] = (rw_gate * _sigmoid(rw_gate)).astype(BF16)
    project(o_mg, RW_COLS + G_COLS + MLA_COLS, MG_COLS // 2, MG_COLS)


def _in_proj(x2, gain, w_all, v_first, rprm, gprm, seq, tm):
    has_vres = v_first is not None
    n = x2.shape[0]
    rowspec = lambda c: pl.BlockSpec((tm, c), lambda i: (i, 0))
    args = [x2, gain, w_all]
    specs = [rowspec(D_MODEL), _const_spec((1, D_MODEL)),
             pl.BlockSpec((D_MODEL, ALL_COLS), lambda i: (0, 0), pipeline_mode=pl.Buffered(1))]
    if has_vres:
        args.append(v_first)
        specs.append(rowspec(RWKV_WIDTH))
    for nm in RW_PARAMS + (RW_VRES_PARAMS if has_vres else ()):
        args.append(rprm[nm])
        specs.append(_const_spec(rprm[nm].shape))
    for nm in ("ln_w", "ln_b", "w_s", "bias"):
        args.append(gprm[nm])
        specs.append(_const_spec(gprm[nm].shape))
    wide = lambda dt: (rowspec(RWKV_WIDTH), jax.ShapeDtypeStruct((n, RWKV_WIDTH), dt))
    outs = [wide(BF16) for _ in range(RW_PREP_OUTS)]
    outs.append((pl.BlockSpec((PC_ROWS, RWKV_WIDTH), lambda i: (i, 0)),
                 jax.ShapeDtypeStruct((n // tm * PC_ROWS, RWKV_WIDTH), F32)))
    outs += [wide(F32), wide(BF16)]
    if not has_vres:
        outs.append(wide(F32))
    outs += [(rowspec(c), jax.ShapeDtypeStruct((n, c), BF16)) for c in (GMLP_WIDTH, MLA_COLS, MG_COLS)]
    res = pl.pallas_call(
        functools.partial(_in_proj_kernel, has_vres, tm, seq // tm),
        grid=(n // tm,),
        in_specs=specs,
        out_specs=[o[0] for o in outs],
        out_shape=[o[1] for o in outs],
        scratch_shapes=[pltpu.VMEM((8, RWKV_SHIFT_COLS), F32)],
        compiler_params=pltpu.CompilerParams(dimension_semantics=("arbitrary",), vmem_limit_bytes=VMEM_LIMIT),
        name="in_proj_vres" if has_vres else "in_proj",
    )(*args)
    nrw = RW_PREP_OUTS + 3 + (0 if has_vres else 1)
    return res[:nrw], res[nrw], res[nrw + 1], res[nrw + 2]


def _rwkv_kernel(tc, *refs):
    it = iter(refs)
    prepared = [next(it) for _ in range(RW_PREP_OUTS)]
    pc_ref, bonus_ref, gate_ref, lnw_ref, lnb_ref, hsum_ref = (next(it) for _ in range(6))
    y_ref = next(it)
    state_ref = next(it)
    stacked = [next(it) for _ in range(RW_PREP_OUTS)]
    pc_s, y_s = next(it), next(it)

    @pl.when(pl.program_id(1) == 0)
    def _():
        state_ref[...] = jnp.zeros_like(state_ref)

    nsub = tc // SUB
    npair = RWKV_HEADS // 2

    def pieces(x):
        for c in range(nsub):
            for j in range(npair):
                yield c * npair + j, x[SUB * c:SUB * (c + 1), LANES * j:LANES * (j + 1)]

    lane = lax.broadcasted_iota(jnp.int32, (tc, RWKV_WIDTH), 1)
    first16 = jnp.where((lane % LANES) < RWKV_HEAD_DIM, 1.0, 0.0).astype(BF16)
    for src, dst in zip(prepared, stacked):
        x16 = src[...]
        xa = x16 * first16
        xb = x16 - xa
        for b, piece in pieces(xa):
            dst[b, 0:SUB, :] = piece
        for b, piece in pieces(xb):
            dst[b, SUB:PAIR, :] = piece
    pc = pc_ref[...]
    for c in range(nsub):
        for j in range(npair):
            pc_s[c * npair + j] = jnp.broadcast_to(pc[c:c + 1, LANES * j:LANES * (j + 1)], (PAIR, LANES))

    ri = lax.broadcasted_iota(jnp.int32, (PAIR, PAIR), 0)
    ci = lax.broadcasted_iota(jnp.int32, (PAIR, PAIR), 1)
    strict = ri > ci
    incl = ri >= ci
    eye = jnp.where(ri == ci, 1.0, 0.0).astype(F32)
    levels = (4, 8, 16, 32, 64)
    lvl_mask = [strict & ((ri // b) == (ci // b)) & ((ri // (b // 2)) != (ci // (b // 2))) for b in levels]
    lvl2 = strict & ((ri // 2) == (ci // 2))

    at, rt, bt, kt, vv, bd, kd = (s[...] for s in stacked)
    s = _bdot_nt(jnp.concatenate([at, rt], axis=1), jnp.concatenate([bt, kt], axis=1))
    nab = jnp.where(strict, s[:, 0:PAIR, 0:PAIR], 0.0)
    aak = jnp.where(strict, s[:, 0:PAIR, PAIR:2 * PAIR], 0.0).astype(BF16)
    rb = jnp.where(incl, s[:, PAIR:2 * PAIR, 0:PAIR], 0.0).astype(BF16)
    rk = jnp.where(incl, s[:, PAIR:2 * PAIR, PAIR:2 * PAIR], 0.0).astype(BF16)
    d = eye + jnp.where(lvl2, nab, 0.0)
    for m in lvl_mask:
        e = jnp.where(m, nab, 0.0).astype(BF16)
        db = d.astype(BF16)
        d = d + _bdot(_bdot(db, e).astype(BF16), db)
    t = d.astype(BF16)
    w1 = _bdot(aak, vv).astype(BF16)
    au = _bdot(t, jnp.concatenate([at, w1], axis=2))
    ah = au[:, :, 0:PAIR].astype(BF16)
    uh = au[:, :, PAIR:2 * PAIR]
    kv = _bdot_tn(kd, vv)
    pcm = jnp.swapaxes(pc_s[...], 1, 2)
    rbk = jnp.concatenate([rb, rk], axis=2)

    h = state_ref[...]
    for c in range(nsub):
        sl = slice(c * npair, (c + 1) * npair)
        arh = _bdot(jnp.concatenate([ah[sl], rt[sl]], axis=1), h.astype(BF16))
        u = (arh[:, 0:PAIR] + uh[sl]).astype(BF16)
        h = pcm[sl] * h + _bdot_tn(bd[sl], u) + kv[sl]
        y = arh[:, PAIR:2 * PAIR] + _bdot(rbk[sl], jnp.concatenate([u, vv[sl]], axis=1))
        for j in range(npair):
            y_s[SUB * c:SUB * (c + 1), LANES * j:LANES * (j + 1)] = y[j, 0:SUB] + y[j, SUB:PAIR]
    state_ref[...] = h

    y = y_s[...]
    hsum = hsum_ref[...]
    inv_n = 1.0 / RWKV_HEAD_DIM
    mean = _head_sum(y, hsum) * inv_n
    yc = y - mean
    var = _head_sum(yc * yc, hsum) * inv_n
    y = yc * lax.rsqrt(var + RWKV_GN_EPS) * lnw_ref[...] + lnb_ref[...]
    y_ref[...] = ((y + bonus_ref[...]) * gate_ref[...].astype(F32)).astype(y_ref.dtype)


def _rwkv(prepared, prm, bsz, seq, tc):
    n = bsz * seq
    nt = seq // tc
    row = lambda b, t: (b * nt + t, 0)
    wide = pl.BlockSpec((tc, RWKV_WIDTH), row)
    specs = [wide] * RW_PREP_OUTS + [pl.BlockSpec((PC_ROWS, RWKV_WIDTH), row), wide, wide]
    args = list(prepared[:RW_PREP_OUTS + 3])
    for nm in ("ln_w", "ln_b", "hsum"):
        args.append(prm[nm])
        specs.append(_const_spec(prm[nm].shape))
    nb = (tc // SUB) * (RWKV_HEADS // 2)
    scratch = [pltpu.VMEM((RWKV_HEADS // 2, PAIR, PAIR), F32)]
    scratch += [pltpu.VMEM((nb, PAIR, PAIR), BF16) for _ in range(RW_PREP_OUTS)]
    scratch += [pltpu.VMEM((nb, PAIR, PAIR), F32), pltpu.VMEM((tc, RWKV_WIDTH), F32)]
    return pl.pallas_call(
        functools.partial(_rwkv_kernel, tc),
        grid=(bsz, nt),
        in_specs=specs,
        out_specs=wide,
        out_shape=jax.ShapeDtypeStruct((n, RWKV_WIDTH), BF16),
        scratch_shapes=scratch,
        compiler_params=pltpu.CompilerParams(dimension_semantics=("arbitrary", "arbitrary"),
                                             vmem_limit_bytes=VMEM_LIMIT),
        name="rwkv",
    )(*args)


def _gelu(x):
    return 0.5 * x * (1.0 + lax.erf(x * np.float32(1.0 / np.sqrt(2.0))))


def _gmlp_pointwise(u, v, gate, lnw_ref, lnb_ref):
    u = _gelu(u)
    v = _gelu(v)
    mu = jnp.mean(v, axis=-1, keepdims=True)
    vc = v - mu
    var = jnp.mean(vc * vc, axis=-1, keepdims=True)
    v = vc * lax.rsqrt(var + GMLP_LN_EPS) * lnw_ref[...] + lnb_ref[...]
    return u * gate * _sigmoid(gate), v


def _gmlp_position_mix(tm, state, ws_ref, bias_ref, o_ref):
    ug, v = state
    lane = lax.broadcasted_iota(jnp.int32, (GMLP_CHUNK, LANES), 1)
    first = lane < (LANES // 2)
    ri = lax.broadcasted_iota(jnp.int32, (GMLP_CHUNK, 2 * GMLP_CHUNK), 0)
    ci = lax.broadcasted_iota(jnp.int32, (GMLP_CHUNK, 2 * GMLP_CHUNK), 1)
    causal = ri >= (ci % GMLP_CHUNK)
    for j in range(GMLP_GROUPS // 2):
        w = jnp.where(causal, ws_ref[j], 0.0).astype(BF16)
        ls = slice(LANES * j, LANES * (j + 1))
        for c in range(tm // GMLP_CHUNK):
            rs = slice(GMLP_CHUNK * c, GMLP_CHUNK * (c + 1))
            vp = v[rs, ls]
            stacked = jnp.concatenate([jnp.where(first, vp, 0.0), jnp.where(first, 0.0, vp)], axis=0).astype(BF16)
            vm = _dot(w, stacked) + bias_ref[:, ls]
            o_ref[rs, ls] = (ug[rs, ls] * vm).astype(o_ref.dtype)


def _rms(x, gain):
    return x * lax.rsqrt(jnp.mean(x * x, axis=-1, keepdims=True) + NORM_EPS) * gain


def _rope_table_kernel(pos_ref, invf_ref, o_ref):
    ang = pos_ref[...].astype(F32) * invf_ref[...]
    lane = lax.broadcasted_iota(jnp.int32, ang.shape, 1)
    o_ref[...] = jnp.where((lane % MLA_ROPE) < MLA_ROPE // 2, jnp.cos(ang), jnp.sin(ang))


def _rope_table(positions, inv_freq):
    n = positions.size
    per_row = LANES // MLA_ROPE
    pos = jnp.repeat(positions.reshape(n // per_row, per_row), MLA_ROPE, axis=1)
    invf = jnp.tile(inv_freq, LANES // inv_freq.size).reshape(1, LANES)
    rows = n // per_row
    tr = min(1024, rows)
    tab = pl.pallas_call(
        _rope_table_kernel,
        grid=(rows // tr,),
        in_specs=[pl.BlockSpec((tr, LANES), lambda i: (i, 0)), _const_spec((1, LANES))],
        out_specs=pl.BlockSpec((tr, LANES), lambda i: (i, 0)),
        out_shape=jax.ShapeDtypeStruct((rows, LANES), F32),
        name="rope_table",
    )(pos, invf)
    return tab.reshape(n, MLA_ROPE)


def _rope_spread():
    half = MLA_ROPE // 2
    m = np.zeros((MLA_ROPE, 2 * LANES), np.float32)
    for i in range(half):
        m[i, ROPE_LANE0 + i] = 1.0
        m[i, ROPE_LANE0 + half + i] = 1.0
        m[half + i, LANES + ROPE_LANE0 + i] = -1.0
        m[half + i, LANES + ROPE_LANE0 + half + i] = 1.0
    return jnp.asarray(m, BF16)


def _with_swapped_rope(w):
    half = MLA_ROPE // 2
    lo, hi = ROPE_LANE0, ROPE_LANE0 + MLA_ROPE
    return jnp.concatenate([w[..., :hi], w[..., lo + half:hi], w[..., lo:lo + half]], axis=-1)


def _mla_prep_kernel(m_ref, cs_ref, spread_ref, qn_ref, wq_ref, kvn_ref, wkv_ref, q_ref, k_ref, vt_ref):
    c0 = MLA_WIDTH
    c_q = m_ref[:, c0:c0 + MLA_Q_LORA].astype(F32)
    c_kv = m_ref[:, c0 + MLA_Q_LORA:c0 + MLA_Q_LORA + MLA_KV_LORA].astype(F32)
    k_rope = m_ref[:, c0 + MLA_Q_LORA + MLA_KV_LORA:MLA_COLS].astype(F32)
    tab = _dot_exact_rhs(cs_ref[...], spread_ref[...])
    lane = lax.broadcasted_iota(jnp.int32, (1, LANES), 1)
    cos = tab[:, 0:LANES] + jnp.where(lane < ROPE_LANE0, 1.0, 0.0)
    sin = tab[:, LANES:2 * LANES]

    def rope(x):
        return x * cos + pltpu.roll(x, LANES - MLA_ROPE, axis=1) * sin

    scale = np.float32((MLA_NOPE + MLA_ROPE) ** -0.5 * np.log2(np.e))
    q = _dot(_rms(c_q, qn_ref[...]).astype(BF16), wq_ref[...])
    kv = _dot(_rms(c_kv, kvn_ref[...]).astype(BF16), wkv_ref[...])
    kr = rope(k_rope)
    for h in range(MLA_HEADS):
        ls = slice(HEAD_PAD * h, HEAD_PAD * (h + 1))
        q_ref[:, ls] = (rope(q[:, ls]) * scale).astype(BF16)
        k_ref[:, ls] = (kv[:, ls] + kr).astype(BF16)
    vl = lax.broadcasted_iota(jnp.int32, (1, HEAD_PAD), 1)
    for h in range(MLA_HEADS):
        vh = kv[:, (MLA_HEADS + h) * HEAD_PAD:(MLA_HEADS + h + 1) * HEAD_PAD]
        vt_ref[0, h, 0] = jnp.where(vl == MLA_V, 1.0, vh).T.astype(BF16)


def _mla_prep(o_mla, rope_tab, prm, bsz, seq, tm):
    n = o_mla.shape[0]
    nt = seq // tm
    wide = MLA_HEADS * HEAD_PAD
    return pl.pallas_call(
        _mla_prep_kernel,
        grid=(n // tm,),
        in_specs=[pl.BlockSpec((tm, MLA_COLS), lambda i: (i, 0)),
                  pl.BlockSpec((tm, MLA_ROPE), lambda i: (i, 0)),
                  _const_spec(prm["spread"].shape),
                  _const_spec((1, MLA_Q_LORA)), _const_spec(prm["w_uq"].shape),
                  _const_spec((1, MLA_KV_LORA)), _const_spec(prm["w_ukv"].shape)],
        out_specs=[pl.BlockSpec((tm, wide), lambda i: (i, 0)), pl.BlockSpec((tm, wide), lambda i: (i, 0)),
                   pl.BlockSpec((1, MLA_HEADS, 1, HEAD_PAD, tm), lambda i: (i // nt, 0, i % nt, 0, 0))],
        out_shape=[jax.ShapeDtypeStruct((n, wide), BF16), jax.ShapeDtypeStruct((n, wide), BF16),
                   jax.ShapeDtypeStruct((bsz, MLA_HEADS, nt, HEAD_PAD, tm), BF16)],
        compiler_params=pltpu.CompilerParams(dimension_semantics=("arbitrary",), vmem_limit_bytes=VMEM_LIMIT),
        name="mla_prep",
    )(o_mla, rope_tab, prm["spread"], prm["q_norm"], prm["w_uq"], prm["kv_norm"], prm["w_ukv"])


ATTN_HEADS_PER_STEP = 4
V_ROWS = 80


def _attn_kernel(tq, nh, q_ref, k_ref, vt_ref, gate_ref, o_ref, m_s, acc_s):
    i = pl.program_id(2)
    ki = lax.broadcasted_iota(jnp.int32, (tq, tq), 0)
    qi = lax.broadcasted_iota(jnp.int32, (tq, tq), 1)
    causal = ki <= qi
    heads = [slice(HEAD_PAD * hh, HEAD_PAD * (hh + 1)) for hh in range(nh)]
    m_s[...] = jnp.full(m_s.shape, -jnp.inf, F32)
    acc_s[...] = jnp.zeros(acc_s.shape, F32)
    groups = tq // 8

    def block(j, masked):
        r0 = pl.multiple_of(j * tq, tq)
        sts = [_dot_nt(k_ref[pl.ds(r0, tq), ls], q_ref[:, ls]) for ls in heads]
        for hh, st in enumerate(sts):
            if masked:
                st = jnp.where(causal, st, -jnp.inf)
            s3 = st.reshape(groups, 8, tq)
            mc = jnp.max(s3, axis=0)
            for sh in (4, 2, 1):
                mc = jnp.maximum(mc, pltpu.roll(mc, sh, axis=0))
            m_old = m_s[hh]
            m_new = jnp.maximum(m_old, mc)
            p = jnp.exp2(s3 - m_new[None]).reshape(tq, tq).astype(BF16)
            alpha = jnp.exp2(m_old - m_new)
            pv = _dot(vt_ref[0, hh, j, 0:V_ROWS, :], p)
            acc_s[hh] = (acc_s[hh].reshape(V_ROWS // 8, 8, tq) * alpha[None]).reshape(V_ROWS, tq) + pv
            m_s[hh] = m_new

    def full_block(j, c):
        block(j, False)
        return c

    lax.fori_loop(0, i, full_block, 0)
    block(i, True)
    outs = []
    for hh in range(nh):
        acc = acc_s[hh]
        outs.append(acc[0:MLA_V] * (1.0 / acc[MLA_V:MLA_V + 1]))
    out = jnp.concatenate(outs, axis=0).T
    gate = gate_ref[...].astype(F32)
    o_ref[...] = (out * gate * _sigmoid(gate)).astype(o_ref.dtype)


def _attn(q, k, vt, o_mla, bsz, seq, tq, nh):
    n = bsz * seq
    nq = seq // tq
    return pl.pallas_call(
        functools.partial(_attn_kernel, tq, nh),
        grid=(bsz, MLA_HEADS // nh, nq),
        in_specs=[pl.BlockSpec((tq, nh * HEAD_PAD), lambda b, h, i: (b * nq + i, h)),
                  pl.BlockSpec((seq, nh * HEAD_PAD), lambda b, h, i: (b, h)),
                  pl.BlockSpec((1, nh, nq, HEAD_PAD, tq), lambda b, h, i: (b, h, 0, 0, 0)),
                  pl.BlockSpec((tq, nh * MLA_V), lambda b, h, i: (b * nq + i, h))],
        out_specs=pl.BlockSpec((tq, nh * MLA_V), lambda b, h, i: (b * nq + i, h)),
        out_shape=jax.ShapeDtypeStruct((n, MLA_WIDTH), BF16),
        scratch_shapes=[pltpu.VMEM((nh, 8, tq), F32), pltpu.VMEM((nh, V_ROWS, tq), F32)],
        compiler_params=pltpu.CompilerParams(dimension_semantics=("arbitrary",) * 3, vmem_limit_bytes=VMEM_LIMIT),
        name="attn",
    )(q, k, vt, o_mla)


def _merge_kernel(ya_ref, yb_ref, yc_ref, mg_ref, x_ref, wb_ref, wo_ref, pn_ref, o_ref):
    ys = (ya_ref[...], yb_ref[...], yc_ref[...])
    halves = []
    for c0 in range(0, D_MODEL, PROJ_CHUNK):
        acc = None
        for nb in range(N_BRANCHES):
            gates = _sigmoid(mg_ref[:, nb * D_MODEL + c0:nb * D_MODEL + c0 + PROJ_CHUNK].astype(F32))
            term = gates * _dot(ys[nb], wb_ref[nb, :, c0:c0 + PROJ_CHUNK])
            acc = term if acc is None else acc + term
        halves.append(acc.astype(BF16))
    mixed = _dot(jnp.concatenate(halves, axis=1), wo_ref[...])
    ms = jnp.mean(mixed * mixed, axis=-1, keepdims=True)
    o_ref[...] = x_ref[...] + mixed * lax.rsqrt(ms + NORM_EPS) * pn_ref[...]


def _merge(ya, yb, yc, o_mg, x2, w_branch, w_out, post_norm, tm):
    n = x2.shape[0]
    rowspec = lambda c: pl.BlockSpec((tm, c), lambda i: (i, 0))
    return pl.pallas_call(
        _merge_kernel,
        grid=(n // tm,),
        in_specs=[rowspec(BRANCH_WIDTH), rowspec(BRANCH_WIDTH), rowspec(BRANCH_WIDTH), rowspec(MG_COLS),
                  rowspec(D_MODEL), _const_spec(w_branch.shape), _const_spec(w_out.shape), _const_spec((1, D_MODEL))],
        out_specs=rowspec(D_MODEL),
        out_shape=jax.ShapeDtypeStruct((n, D_MODEL), F32),
        compiler_params=pltpu.CompilerParams(dimension_semantics=("arbitrary",), vmem_limit_bytes=VMEM_LIMIT),
        name="merge",
    )(ya, yb, yc, o_mg, x2, w_branch, w_out, post_norm)


def _row(v):
    return v.reshape(1, -1).astype(F32)


def _pad_rows(w, rows, at):
    out = jnp.zeros((rows, w.shape[1]), w.dtype)
    return out.at[at:at + w.shape[0]].set(w)


IN_SEGS = (RWKV_SHIFT_COLS, RWKV_WIDTH, GMLP_WIDTH, GMLP_WIDTH, GMLP_WIDTH,
           MLA_Q_LORA, MLA_KV_LORA, MLA_ROPE, MLA_WIDTH, N_BRANCHES * D_MODEL)
W_ROWS = 128


def _w_layout_kernel(has_vres, *refs):
    w_ref = refs[0]
    vd_ref = refs[1] if has_vres else None
    o_ref = refs[-1]
    o = [int(c) for c in np.cumsum((0,) + IN_SEGS)]
    cols = lambda c0, c1: w_ref[:, c0:c1]
    seg = lambda i: cols(o[i], o[i + 1])
    rows = w_ref.shape[0]
    zeros = lambda c: jnp.zeros((rows, c), F32)
    half = MLA_ROPE // 2
    k_rope = seg(7)
    misc_a = jnp.concatenate([vd_ref[...], zeros(MISC - RWKV_VRES_LORA)], axis=1) if has_vres else zeros(MISC)
    misc_b = jnp.concatenate([zeros(ROPE_LANE0), k_rope, k_rope[:, half:], k_rope[:, :half]], axis=1)
    col = 0
    for piece in (cols(o[0], o[2]), misc_a, cols(o[2], o[5]), seg(8), seg(5), seg(6), misc_b, seg(9)):
        o_ref[:, col:col + piece.shape[1]] = piece.astype(BF16)
        col += piece.shape[1]


def _w_layout(l, w_in, v_down):
    has_vres = l > 0
    blocks = D_MODEL // W_ROWS
    args = [w_in.reshape(-1, w_in.shape[2])] + ([v_down.reshape(-1, RWKV_VRES_LORA)] if has_vres else [])
    specs = [pl.BlockSpec((W_ROWS, w_in.shape[2]), lambda i: (l * blocks + i, 0))]
    if has_vres:
        specs.append(pl.BlockSpec((W_ROWS, RWKV_VRES_LORA), lambda i: ((l - 1) * blocks + i, 0)))
    return pl.pallas_call(
        functools.partial(_w_layout_kernel, has_vres),
        grid=(blocks,),
        in_specs=specs,
        out_specs=pl.BlockSpec((W_ROWS, ALL_COLS), lambda i: (i, 0)),
        out_shape=jax.ShapeDtypeStruct((D_MODEL, ALL_COLS), BF16),
        compiler_params=pltpu.CompilerParams(dimension_semantics=("arbitrary",), vmem_limit_bytes=VMEM_LIMIT),
        name="w_layout",
    )(*args)


def _layer_weights(l, w_in, rwkv_v_down, mla_w_uq, mla_w_ukv):
    w_all = _w_layout(l, w_in, rwkv_v_down)
    qd = MLA_NOPE + MLA_ROPE
    wq = mla_w_uq[l].reshape(MLA_Q_LORA, MLA_HEADS, qd)
    wq = _with_swapped_rope(jnp.pad(wq, ((0, 0), (0, 0), (0, HEAD_PAD - qd)))).reshape(MLA_Q_LORA, MLA_HEADS * HEAD_PAD)
    wkv = mla_w_ukv[l].reshape(MLA_KV_LORA, MLA_HEADS, MLA_NOPE + MLA_V)
    wk = jnp.pad(wkv[:, :, :MLA_NOPE], ((0, 0), (0, 0), (0, HEAD_PAD - MLA_NOPE)))
    wv = jnp.pad(wkv[:, :, MLA_NOPE:], ((0, 0), (0, 0), (0, HEAD_PAD - MLA_V)))
    wkv_all = jnp.concatenate([wk.reshape(MLA_KV_LORA, -1), wv.reshape(MLA_KV_LORA, -1)], axis=1)
    return w_all, wq.astype(BF16), wkv_all.astype(BF16)


def _constants(tc):
    head = np.arange(LANES) // RWKV_HEAD_DIM
    hsum = (head[:, None] == head[None, :]).astype(np.float32)
    t = np.arange(tc)
    same = (t[:, None] // SUB) == (t[None, :] // SUB)
    cum = np.concatenate([same & (t[:, None] >= t[None, :]), same], axis=0).astype(np.float32)
    return {"hsum": jnp.asarray(hsum, BF16), "cum": jnp.asarray(cum, BF16)}


def kernel(x, positions, w_in, pre_norm, post_norm, rwkv_mu, rwkv_w0, rwkv_w_up, rwkv_a0, rwkv_a_up, rwkv_k_k, rwkv_k_a, rwkv_r_k, rwkv_ln_w, rwkv_ln_b, rwkv_v0, rwkv_v_down, rwkv_v_up, gmlp_ln_w, gmlp_ln_b, gmlp_w_s, gmlp_b_s, mla_q_norm, mla_w_uq, mla_kv_norm, mla_w_ukv, w_branch, w_out):
    bsz, seq, _ = x.shape
    n = bsz * seq
    depth = w_in.shape[0]
    tc, tm, tq = min(TIME_TILE, seq), min(ROW_TILE, n), min(ATTN_TILE, seq)
    consts = _constants(tc)
    inv_freq = 1.0 / (ROPE_THETA ** (jnp.arange(0, MLA_ROPE, 2, dtype=F32) / MLA_ROPE))
    rope_tab = _rope_table(positions, inv_freq)
    spread = _rope_spread()

    x2 = x.reshape(n, D_MODEL)
    v_first = None
    for l in range(depth):
        w_all, wq, wkv = _layer_weights(l, w_in, rwkv_v_down, mla_w_uq, mla_w_ukv)
        ws = gmlp_w_s[l].reshape(GMLP_GROUPS // 2, 2, GMLP_CHUNK, GMLP_CHUNK)
        gprm = {"ln_w": _row(gmlp_ln_w[l]), "ln_b": _row(gmlp_ln_b[l]),
                "w_s": jnp.concatenate([ws[:, 0], ws[:, 1]], axis=-1),
                "bias": jnp.repeat(gmlp_b_s[l].T, GMLP_WIDTH // GMLP_GROUPS, axis=1)}
        rprm = {"mu": _row(rwkv_mu[l]), "w0": _row(rwkv_w0[l]), "a0": _row(rwkv_a0[l]),
                "w_up": _pad_rows(rwkv_w_up[l], LANES, 0).astype(BF16),
                "a_up": _pad_rows(rwkv_a_up[l], LANES, RWKV_DECAY_LORA).astype(BF16),
                "k_k": _row(rwkv_k_k[l]), "k_a": _row(rwkv_k_a[l]), "r_k": _row(rwkv_r_k[l]),
                "ln_w": _row(rwkv_ln_w[l]), "ln_b": _row(rwkv_ln_b[l]), "hsum": consts["hsum"], "cum": consts["cum"]}
        if l > 0:
            rprm["v0"] = _row(rwkv_v0[l - 1])
            rprm["v_up"] = _pad_rows(rwkv_v_up[l - 1], MISC, 0).astype(BF16)
        prepared, y_b, o_mla, o_mg = _in_proj(x2, _row(pre_norm[l]), w_all, v_first, rprm, gprm, seq, tc)
        if l == 0:
            v_first = prepared[-1]
        y_a = _rwkv(prepared, rprm, bsz, seq, tc)

        mprm = {"spread": spread, "q_norm": _row(mla_q_norm[l]), "w_uq": wq,
                "kv_norm": _row(mla_kv_norm[l]), "w_ukv": wkv}
        q, k, vt = _mla_prep(o_mla, rope_tab, mprm, bsz, seq, tq)
        y_c = _attn(q, k, vt, o_mla, bsz, seq, tq, ATTN_HEADS_PER_STEP)

        x2 = _merge(y_a, y_b, y_c, o_mg, x2, w_branch[l].astype(BF16), w_out[l].astype(BF16),
                    _row(post_norm[l]), tm)
    return x2.reshape(bsz, seq, D_MODEL)
```

```python
import functools

import numpy as np
import jax
import jax.numpy as jnp
from jax import lax
from jax.experimental import pallas as pl
from jax.experimental.pallas import tpu as pltpu

F32 = jnp.float32
BF16 = jnp.bfloat16

D_MODEL = 1024
BRANCH_WIDTH = 512
N_BRANCHES = 3
NORM_EPS = 1e-6
RWKV_HEADS = 8
RWKV_HEAD_DIM = 64
RWKV_WIDTH = RWKV_HEADS * RWKV_HEAD_DIM
RWKV_DECAY_LORA = 64
RWKV_AAA_LORA = 64
RWKV_VRES_LORA = 32
RWKV_GN_EPS = 64e-5
RWKV_SHIFT_COLS = 3 * RWKV_WIDTH + RWKV_DECAY_LORA + RWKV_AAA_LORA
GMLP_WIDTH = 512
GMLP_GROUPS = 8
GMLP_CHUNK = 128
GMLP_LN_EPS = 1e-5
MLA_HEADS = 8
MLA_Q_LORA = 384
MLA_KV_LORA = 256
MLA_NOPE = 64
MLA_ROPE = 32
MLA_V = 64
MLA_WIDTH = MLA_HEADS * MLA_V
ROPE_THETA = 10000.0

LANES = 128
MISC = LANES
RW_COLS = RWKV_SHIFT_COLS + RWKV_WIDTH + MISC
G_COLS = 3 * GMLP_WIDTH
MLA_COLS = MLA_WIDTH + MLA_Q_LORA + MLA_KV_LORA + MISC
MG_COLS = N_BRANCHES * D_MODEL
ALL_COLS = RW_COLS + G_COLS + MLA_COLS + MG_COLS
HEAD_PAD = LANES
ROPE_LANE0 = MLA_NOPE
SUB = 64
PAIR = 2 * SUB

VMEM_LIMIT = 56 * 1024 * 1024
TIME_TILE = 256
ROW_TILE = 1024
ATTN_TILE = 512


def _dot(a, b):
    return jnp.dot(a, b, preferred_element_type=F32)


def _dot_nt(a, b):
    return lax.dot_general(a, b, (((1,), (1,)), ((), ())), preferred_element_type=F32)


def _bdot(a, b):
    return lax.dot_general(a, b, (((2,), (1,)), ((0,), (0,))), preferred_element_type=F32)


def _bdot_nt(a, b):
    return lax.dot_general(a, b, (((2,), (2,)), ((0,), (0,))), preferred_element_type=F32)


def _bdot_tn(a, b):
    return lax.dot_general(a, b, (((1,), (1,)), ((0,), (0,))), preferred_element_type=F32)


def _split3(x):
    hi = x.astype(BF16)
    r1 = x - hi.astype(F32)
    mid = r1.astype(BF16)
    lo = (r1 - mid.astype(F32)).astype(BF16)
    return hi, mid, lo


def _dot_split2_lhs(m, x):
    hi = x.astype(BF16)
    lo = (x - hi.astype(F32)).astype(BF16)
    return _dot(m, hi) + _dot(m, lo)


def _dot_exact_rhs(x, m):
    hi, mid, lo = _split3(x)
    return _dot(hi, m) + _dot(mid, m) + _dot(lo, m)


def _head_sum(x, m):
    x16 = x.astype(BF16)
    return jnp.concatenate([_dot(x16[:, c:c + LANES], m) for c in range(0, x.shape[1], LANES)], axis=1)


def _sigmoid(x):
    return 0.5 * jnp.tanh(0.5 * x) + 0.5


def _const_spec(shape):
    nd = len(shape)
    return pl.BlockSpec(shape, lambda *_: (0,) * nd)


RW_PREP_OUTS = 7
PROJ_CHUNK = 512
PC_ROWS = 8
RW_PARAMS = ("mu", "w0", "w_up", "a0", "a_up", "k_k", "k_a", "r_k", "hsum", "cum")
RW_VRES_PARAMS = ("v0", "v_up")


def _in_proj_kernel(has_vres, tm, tiles_per_row, *refs):
    it = iter(refs)
    x_ref, g_ref, w_ref = next(it), next(it), next(it)
    vf_ref = next(it) if has_vres else None
    prm = {nm: next(it) for nm in RW_PARAMS + (RW_VRES_PARAMS if has_vres else ())}
    lnw_ref, lnb_ref, ws_ref, bias_ref = (next(it) for _ in range(4))
    at_o, rt_o, bt_o, kt_o, v_o, bd_o, kd_o, pc_o, bonus_o, gate_o = (next(it) for _ in range(RW_PREP_OUTS + 3))
    vout_o = None if has_vres else next(it)
    o_yb, o_mla, o_mg = next(it), next(it), next(it)
    carry_ref = next(it)

    x = x_ref[...]
    ms = jnp.mean(x * x, axis=-1, keepdims=True)
    h = (x * lax.rsqrt(ms + NORM_EPS) * g_ref[...]).astype(BF16)

    def columns(c0, c1):
        return _dot(h, w_ref[:, c0:c1])

    def project(o, col, c_from, c_to):
        for c0 in range(c_from, c_to, PROJ_CHUNK):
            cw = min(PROJ_CHUNK, c_to - c0)
            o[:, c0:c0 + cw] = columns(col + c0, col + c0 + cw).astype(o.dtype)

    p = jnp.concatenate([columns(c0, min(c0 + PROJ_CHUNK, RWKV_SHIFT_COLS))
                         for c0 in range(0, RWKV_SHIFT_COLS, PROJ_CHUNK)], axis=1)
    rw_gate = columns(RWKV_SHIFT_COLS, RWKV_SHIFT_COLS + RWKV_WIDTH)
    vd = columns(RWKV_SHIFT_COLS + RWKV_WIDTH, RW_COLS) if has_vres else None

    first_tile = (pl.program_id(0) % tiles_per_row) == 0
    row = lax.broadcasted_iota(jnp.int32, p.shape, 0)
    before = jnp.where(first_tile, 0.0, carry_ref[0:1, :])
    shifted = jnp.where(row == 0, before, pltpu.roll(p, 1, axis=0))
    carry_ref[0:1, :] = p[tm - 1:tm, :]
    p = p + (shifted - p) * prm["mu"][...]
    r = p[:, 0:RWKV_WIDTH]
    k = p[:, RWKV_WIDTH:2 * RWKV_WIDTH]
    v = p[:, 2 * RWKV_WIDTH:3 * RWKV_WIDTH]
    lora_in = p[:, 3 * RWKV_WIDTH:RWKV_SHIFT_COLS]
    lora_w = jnp.tanh(lora_in).astype(BF16)
    lora_a = lora_in.astype(BF16)
    hsum = prm["hsum"][...]
    kk = k * prm["k_k"][...]
    kk_sq = (kk * kk)

    gu, gv, gg = (columns(RW_COLS + i * GMLP_WIDTH, RW_COLS + (i + 1) * GMLP_WIDTH) for i in range(3))
    z = -(prm["w0"][...] + _dot(lora_w, prm["w_up"][...]))
    a_pre = prm["a0"][...] + _dot(lora_a, prm["a_up"][...])
    vg_pre = prm["v0"][...] + _dot(vd.astype(BF16), prm["v_up"][...]) if has_vres else None
    kk_norm2 = _head_sum(kk_sq, hsum)

    softplus = jnp.maximum(z, 0.0) + jnp.log1p(jnp.exp(-jnp.abs(z)))
    lw = -jnp.exp(-softplus - 0.5)
    a = _sigmoid(a_pre)
    if has_vres:
        v = v + (vf_ref[...] - v) * _sigmoid(vg_pre)
    else:
        vout_o[...] = v
    kk = kk * lax.rsqrt(jnp.maximum(kk_norm2, 1e-24))
    k2 = k * (1.0 + (a - 1.0) * prm["k_a"][...])
    av = -kk
    bv = kk * a
    rkk = r * k2 * prm["r_k"][...]
    gmlp_state = _gmlp_pointwise(gu, gv, gg, lnw_ref, lnb_ref)

    project(o_mla, RW_COLS + G_COLS, 0, MLA_COLS)
    gs = _dot_split2_lhs(prm["cum"][...], lw)
    bonus = _head_sum(rkk, hsum)
    _gmlp_position_mix(tm, gmlp_state, ws_ref, bias_ref, o_yb)

    project(o_mg, RW_COLS + G_COLS + MLA_COLS, 0, MG_COLS // 2)
    g = gs[0:tm]
    gl = gs[tm:2 * tm]
    e_neg = jnp.exp(-g)
    e_end = jnp.exp(gl - g)
    at_o[...] = (av * jnp.exp(g - lw)).astype(BF16)
    rt_o[...] = (r * jnp.exp(g)).astype(BF16)
    bt_o[...] = (bv * e_neg).astype(BF16)
    kt_o[...] = (k2 * e_neg).astype(BF16)
    v_o[...] = v.astype(BF16)
    bd_o[...] = (bv * e_end).astype(BF16)
    kd_o[...] = (k2 * e_end).astype(BF16)
    pc = jnp.exp(gl)
    nsub = tm // SUB
    pc_o[...] = jnp.concatenate([pc[SUB * c:SUB * c + 1] for c in range(nsub)]
                                + [jnp.zeros((PC_ROWS - nsub, RWKV_WIDTH), F32)], axis=0)
    bonus_o[...] = bonus * v
    gate_o[...] = (rw_gate * _sigmoid(rw_gate)).astype(BF16)
    project(o_mg, RW_COLS + G_COLS + MLA_COLS, MG_COLS // 2, MG_COLS)


def _in_proj(x2, gain, w_all, v_first, rprm, gprm, seq, tm):
    has_vres = v_first is not None
    n = x2.shape[0]
    rowspec = lambda c: pl.BlockSpec((tm, c), lambda i: (i, 0))
    args = [x2, gain, w_all]
    specs = [rowspec(D_MODEL), _const_spec((1, D_MODEL)),
             pl.BlockSpec((D_MODEL, ALL_COLS), lambda i: (0, 0), pipeline_mode=pl.Buffered(1))]
    if has_vres:
        args.append(v_first)
        specs.append(rowspec(RWKV_WIDTH))
    for nm in RW_PARAMS + (RW_VRES_PARAMS if has_vres else ()):
        args.append(rprm[nm])
        specs.append(_const_spec(rprm[nm].shape))
    for nm in ("ln_w", "ln_b", "w_s", "bias"):
        args.append(gprm[nm])
        specs.append(_const_spec(gprm[nm].shape))
    wide = lambda dt: (rowspec(RWKV_WIDTH), jax.ShapeDtypeStruct((n, RWKV_WIDTH), dt))
    outs = [wide(BF16) for _ in range(RW_PREP_OUTS)]
    outs.append((pl.BlockSpec((PC_ROWS, RWKV_WIDTH), lambda i: (i, 0)),
                 jax.ShapeDtypeStruct((n // tm * PC_ROWS, RWKV_WIDTH), F32)))
    outs += [wide(F32), wide(BF16)]
    if not has_vres:
        outs.append(wide(F32))
    outs += [(rowspec(c), jax.ShapeDtypeStruct((n, c), BF16)) for c in (GMLP_WIDTH, MLA_COLS, MG_COLS)]
    res = pl.pallas_call(
        functools.partial(_in_proj_kernel, has_vres, tm, seq // tm),
        grid=(n // tm,),
        in_specs=specs,
        out_specs=[o[0] for o in outs],
        out_shape=[o[1] for o in outs],
        scratch_shapes=[pltpu.VMEM((8, RWKV_SHIFT_COLS), F32)],
        compiler_params=pltpu.CompilerParams(dimension_semantics=("arbitrary",), vmem_limit_bytes=VMEM_LIMIT),
        name="in_proj_vres" if has_vres else "in_proj",
    )(*args)
    nrw = RW_PREP_OUTS + 3 + (0 if has_vres else 1)
    return res[:nrw], res[nrw], res[nrw + 1], res[nrw + 2]


def _rwkv_kernel(tc, *refs):
    it = iter(refs)
    prepared = [next(it) for _ in range(RW_PREP_OUTS)]
    pc_ref, bonus_ref, gate_ref, lnw_ref, lnb_ref, hsum_ref = (next(it) for _ in range(6))
    y_ref = next(it)
    state_ref = next(it)
    stacked = [next(it) for _ in range(RW_PREP_OUTS)]
    pc_s, y_s = next(it), next(it)

    @pl.when(pl.program_id(1) == 0)
    def _():
        state_ref[...] = jnp.zeros_like(state_ref)

    nsub = tc // SUB
    npair = RWKV_HEADS // 2

    def pieces(x):
        for c in range(nsub):
            for j in range(npair):
                yield c * npair + j, x[SUB * c:SUB * (c + 1), LANES * j:LANES * (j + 1)]

    lane = lax.broadcasted_iota(jnp.int32, (tc, RWKV_WIDTH), 1)
    first16 = jnp.where((lane % LANES) < RWKV_HEAD_DIM, 1.0, 0.0).astype(BF16)
    for src, dst in zip(prepared, stacked):
        x16 = src[...]
        xa = x16 * first16
        xb = x16 - xa
        for b, piece in pieces(xa):
            dst[b, 0:SUB, :] = piece
        for b, piece in pieces(xb):
            dst[b, SUB:PAIR, :] = piece
    pc = pc_ref[...]
    for c in range(nsub):
        for j in range(npair):
            pc_s[c * npair + j] = jnp.broadcast_to(pc[c:c + 1, LANES * j:LANES * (j + 1)], (PAIR, LANES))

    ri = lax.broadcasted_iota(jnp.int32, (PAIR, PAIR), 0)
    ci = lax.broadcasted_iota(jnp.int32, (PAIR, PAIR), 1)
    strict = ri > ci
    incl = ri >= ci
    eye = jnp.where(ri == ci, 1.0, 0.0).astype(F32)
    levels = (4, 8, 16, 32, 64)
    lvl_mask = [strict & ((ri // b) == (ci // b)) & ((ri // (b // 2)) != (ci // (b // 2))) for b in levels]
    lvl2 = strict & ((ri // 2) == (ci // 2))

    at, rt, bt, kt, vv, bd, kd = (s[...] for s in stacked)
    s = _bdot_nt(jnp.concatenate([at, rt], axis=1), jnp.concatenate([bt, kt], axis=1))
    nab = jnp.where(strict, s[:, 0:PAIR, 0:PAIR], 0.0)
    aak = jnp.where(strict, s[:, 0:PAIR, PAIR:2 * PAIR], 0.0).astype(BF16)
    rb = jnp.where(incl, s[:, PAIR:2 * PAIR, 0:PAIR], 0.0).astype(BF16)
    rk = jnp.where(incl, s[:, PAIR:2 * PAIR, PAIR:2 * PAIR], 0.0).astype(BF16)
    d = eye + jnp.where(lvl2, nab, 0.0)
    for m in lvl_mask:
        e = jnp.where(m, nab, 0.0).astype(BF16)
        db = d.astype(BF16)
        d = d + _bdot(_bdot(db, e).astype(BF16), db)
    t = d.astype(BF16)
    w1 = _bdot(aak, vv).astype(BF16)
    au = _bdot(t, jnp.concatenate([at, w1], axis=2))
    ah = au[:, :, 0:PAIR].astype(BF16)
    uh = au[:, :, PAIR:2 * PAIR]
    kv = _bdot_tn(kd, vv)
    pcm = jnp.swapaxes(pc_s[...], 1, 2)
    rbk = jnp.concatenate([rb, rk], axis=2)

    h = state_ref[...]
    for c in range(nsub):
        sl = slice(c * npair, (c + 1) * npair)
        arh = _bdot(jnp.concatenate([ah[sl], rt[sl]], axis=1), h.astype(BF16))
        u = (arh[:, 0:PAIR] + uh[sl]).astype(BF16)
        h = pcm[sl] * h + _bdot_tn(bd[sl], u) + kv[sl]
        y = arh[:, PAIR:2 * PAIR] + _bdot(rbk[sl], jnp.concatenate([u, vv[sl]], axis=1))
        for j in range(npair):
            y_s[SUB * c:SUB * (c + 1), LANES * j:LANES * (j + 1)] = y[j, 0:SUB] + y[j, SUB:PAIR]
    state_ref[...] = h

    y = y_s[...]
    hsum = hsum_ref[...]
    inv_n = 1.0 / RWKV_HEAD_DIM
    mean = _head_sum(y, hsum) * inv_n
    yc = y - mean
    var = _head_sum(yc * yc, hsum) * inv_n
    y = yc * lax.rsqrt(var + RWKV_GN_EPS) * lnw_ref[...] + lnb_ref[...]
    y_ref[...] = ((y + bonus_ref[...]) * gate_ref[...].astype(F32)).astype(y_ref.dtype)


def _rwkv(prepared, prm, bsz, seq, tc):
    n = bsz * seq
    nt = seq // tc
    row = lambda b, t: (b * nt + t, 0)
    wide = pl.BlockSpec((tc, RWKV_WIDTH), row)
    specs = [wide] * RW_PREP_OUTS + [pl.BlockSpec((PC_ROWS, RWKV_WIDTH), row), wide, wide]
    args = list(prepared[:RW_PREP_OUTS + 3])
    for nm in ("ln_w", "ln_b", "hsum"):
        args.append(prm[nm])
        specs.append(_const_spec(prm[nm].shape))
    nb = (tc // SUB) * (RWKV_HEADS // 2)
    scratch = [pltpu.VMEM((RWKV_HEADS // 2, PAIR, PAIR), F32)]
    scratch += [pltpu.VMEM((nb, PAIR, PAIR), BF16) for _ in range(RW_PREP_OUTS)]
    scratch += [pltpu.VMEM((nb, PAIR, PAIR), F32), pltpu.VMEM((tc, RWKV_WIDTH), F32)]
    return pl.pallas_call(
        functools.partial(_rwkv_kernel, tc),
        grid=(bsz, nt),
        in_specs=specs,
        out_specs=wide,
        out_shape=jax.ShapeDtypeStruct((n, RWKV_WIDTH), BF16),
        scratch_shapes=scratch,
        compiler_params=pltpu.CompilerParams(dimension_semantics=("arbitrary", "arbitrary"),
                                             vmem_limit_bytes=VMEM_LIMIT),
        name="rwkv",
    )(*args)


def _gelu(x):
    return 0.5 * x * (1.0 + lax.erf(x * np.float32(1.0 / np.sqrt(2.0))))


def _gmlp_pointwise(u, v, gate, lnw_ref, lnb_ref):
    u = _gelu(u)
    v = _gelu(v)
    mu = jnp.mean(v, axis=-1, keepdims=True)
    vc = v - mu
    var = jnp.mean(vc * vc, axis=-1, keepdims=True)
    v = vc * lax.rsqrt(var + GMLP_LN_EPS) * lnw_ref[...] + lnb_ref[...]
    return u * gate * _sigmoid(gate), v


def _gmlp_position_mix(tm, state, ws_ref, bias_ref, o_ref):
    ug, v = state
    lane = lax.broadcasted_iota(jnp.int32, (GMLP_CHUNK, LANES), 1)
    first = lane < (LANES // 2)
    ri = lax.broadcasted_iota(jnp.int32, (GMLP_CHUNK, 2 * GMLP_CHUNK), 0)
    ci = lax.broadcasted_iota(jnp.int32, (GMLP_CHUNK, 2 * GMLP_CHUNK), 1)
    causal = ri >= (ci % GMLP_CHUNK)
    for j in range(GMLP_GROUPS // 2):
        w = jnp.where(causal, ws_ref[j], 0.0).astype(BF16)
        ls = slice(LANES * j, LANES * (j + 1))
        for c in range(tm // GMLP_CHUNK):
            rs = slice(GMLP_CHUNK * c, GMLP_CHUNK * (c + 1))
            vp = v[rs, ls]
            stacked = jnp.concatenate([jnp.where(first, vp, 0.0), jnp.where(first, 0.0, vp)], axis=0).astype(BF16)
            vm = _dot(w, stacked) + bias_ref[:, ls]
            o_ref[rs, ls] = (ug[rs, ls] * vm).astype(o_ref.dtype)


def _rms(x, gain):
    return x * lax.rsqrt(jnp.mean(x * x, axis=-1, keepdims=True) + NORM_EPS) * gain


def _rope_table_kernel(pos_ref, invf_ref, o_ref):
    ang = pos_ref[...].astype(F32) * invf_ref[...]
    lane = lax.broadcasted_iota(jnp.int32, ang.shape, 1)
    o_ref[...] = jnp.where((lane % MLA_ROPE) < MLA_ROPE // 2, jnp.cos(ang), jnp.sin(ang))


def _rope_table(positions, inv_freq):
    n = positions.size
    per_row = LANES // MLA_ROPE
    pos = jnp.repeat(positions.reshape(n // per_row, per_row), MLA_ROPE, axis=1)
    invf = jnp.tile(inv_freq, LANES // inv_freq.size).reshape(1, LANES)
    rows = n // per_row
    tr = min(1024, rows)
    tab = pl.pallas_call(
        _rope_table_kernel,
        grid=(rows // tr,),
        in_specs=[pl.BlockSpec((tr, LANES), lambda i: (i, 0)), _const_spec((1, LANES))],
        out_specs=pl.BlockSpec((tr, LANES), lambda i: (i, 0)),
        out_shape=jax.ShapeDtypeStruct((rows, LANES), F32),
        name="rope_table",
    )(pos, invf)
    return tab.reshape(n, MLA_ROPE)


def _rope_spread():
    half = MLA_ROPE // 2
    m = np.zeros((MLA_ROPE, 2 * LANES), np.float32)
    for i in range(half):
        m[i, ROPE_LANE0 + i] = 1.0
        m[i, ROPE_LANE0 + half + i] = 1.0
        m[half + i, LANES + ROPE_LANE0 + i] = -1.0
        m[half + i, LANES + ROPE_LANE0 + half + i] = 1.0
    return jnp.asarray(m, BF16)


def _with_swapped_rope(w):
    half = MLA_ROPE // 2
    lo, hi = ROPE_LANE0, ROPE_LANE0 + MLA_ROPE
    return jnp.concatenate([w[..., :hi], w[..., lo + half:hi], w[..., lo:lo + half]], axis=-1)


def _mla_prep_kernel(m_ref, cs_ref, spread_ref, qn_ref, wq_ref, kvn_ref, wkv_ref, q_ref, k_ref, vt_ref):
    c0 = MLA_WIDTH
    c_q = m_ref[:, c0:c0 + MLA_Q_LORA].astype(F32)
    c_kv = m_ref[:, c0 + MLA_Q_LORA:c0 + MLA_Q_LORA + MLA_KV_LORA].astype(F32)
    k_rope = m_ref[:, c0 + MLA_Q_LORA + MLA_KV_LORA:MLA_COLS].astype(F32)
    tab = _dot_exact_rhs(cs_ref[...], spread_ref[...])
    lane = lax.broadcasted_iota(jnp.int32, (1, LANES), 1)
    cos = tab[:, 0:LANES] + jnp.where(lane < ROPE_LANE0, 1.0, 0.0)
    sin = tab[:, LANES:2 * LANES]

    def rope(x):
        return x * cos + pltpu.roll(x, LANES - MLA_ROPE, axis=1) * sin

    scale = np.float32((MLA_NOPE + MLA_ROPE) ** -0.5 * np.log2(np.e))
    q = _dot(_rms(c_q, qn_ref[...]).astype(BF16), wq_ref[...])
    kv = _dot(_rms(c_kv, kvn_ref[...]).astype(BF16), wkv_ref[...])
    kr = rope(k_rope)
    for h in range(MLA_HEADS):
        ls = slice(HEAD_PAD * h, HEAD_PAD * (h + 1))
        q_ref[:, ls] = (rope(q[:, ls]) * scale).astype(BF16)
        k_ref[:, ls] = (kv[:, ls] + kr).astype(BF16)
    vl = lax.broadcasted_iota(jnp.int32, (1, HEAD_PAD), 1)
    tv = vt_ref.shape[4]
    for h in range(MLA_HEADS):
        vh = jnp.where(vl == MLA_V, 1.0, kv[:, (MLA_HEADS + h) * HEAD_PAD:(MLA_HEADS + h + 1) * HEAD_PAD])
        for s in range(vt_ref.shape[2]):
            vt_ref[0, h, s] = vh[s * tv:(s + 1) * tv].T.astype(BF16)


def _mla_prep(o_mla, rope_tab, prm, bsz, seq, tm, tv):
    n = o_mla.shape[0]
    nt = seq // tm
    per = tm // tv
    wide = MLA_HEADS * HEAD_PAD
    return pl.pallas_call(
        _mla_prep_kernel,
        grid=(n // tm,),
        in_specs=[pl.BlockSpec((tm, MLA_COLS), lambda i: (i, 0)),
                  pl.BlockSpec((tm, MLA_ROPE), lambda i: (i, 0)),
                  _const_spec(prm["spread"].shape),
                  _const_spec((1, MLA_Q_LORA)), _const_spec(prm["w_uq"].shape),
                  _const_spec((1, MLA_KV_LORA)), _const_spec(prm["w_ukv"].shape)],
        out_specs=[pl.BlockSpec((tm, wide), lambda i: (i, 0)), pl.BlockSpec((tm, wide), lambda i: (i, 0)),
                   pl.BlockSpec((1, MLA_HEADS, per, HEAD_PAD, tv), lambda i: (i // nt, 0, i % nt, 0, 0))],
        out_shape=[jax.ShapeDtypeStruct((n, wide), BF16), jax.ShapeDtypeStruct((n, wide), BF16),
                   jax.ShapeDtypeStruct((bsz, MLA_HEADS, seq // tv, HEAD_PAD, tv), BF16)],
        compiler_params=pltpu.CompilerParams(dimension_semantics=("arbitrary",), vmem_limit_bytes=VMEM_LIMIT),
        name="mla_prep",
    )(o_mla, rope_tab, prm["spread"], prm["q_norm"], prm["w_uq"], prm["kv_norm"], prm["w_ukv"])


ATTN_HEADS_PER_STEP = 4
V_ROWS = 80


def _attn_kernel(tq, nh, q_ref, k_ref, vt_ref, gate_ref, o_ref, m_s, acc_s):
    i = pl.program_id(2)
    ki = lax.broadcasted_iota(jnp.int32, (tq, tq), 0)
    qi = lax.broadcasted_iota(jnp.int32, (tq, tq), 1)
    causal = ki <= qi
    heads = [slice(HEAD_PAD * hh, HEAD_PAD * (hh + 1)) for hh in range(nh)]
    m_s[...] = jnp.full(m_s.shape, -jnp.inf, F32)
    acc_s[...] = jnp.zeros(acc_s.shape, F32)
    groups = tq // 8

    def block(j, masked):
        r0 = pl.multiple_of(j * tq, tq)
        sts = [_dot_nt(k_ref[pl.ds(r0, tq), ls], q_ref[:, ls]) for ls in heads]
        for hh, st in enumerate(sts):
            if masked:
                st = jnp.where(causal, st, -jnp.inf)
            s3 = st.reshape(groups, 8, tq)
            mc = jnp.max(s3, axis=0)
            for sh in (4, 2, 1):
                mc = jnp.maximum(mc, pltpu.roll(mc, sh, axis=0))
            m_old = m_s[hh]
            m_new = jnp.maximum(m_old, mc)
            p = jnp.exp2(s3 - m_new[None]).reshape(tq, tq).astype(BF16)
            alpha = jnp.exp2(m_old - m_new)
            pv = _dot(vt_ref[0, hh, j, 0:V_ROWS, :], p)
            acc_s[hh] = (acc_s[hh].reshape(V_ROWS // 8, 8, tq) * alpha[None]).reshape(V_ROWS, tq) + pv
            m_s[hh] = m_new

    def full_block(j, c):
        block(j, False)
        return c

    lax.fori_loop(0, i, full_block, 0)
    block(i, True)
    outs = []
    for hh in range(nh):
        acc = acc_s[hh]
        outs.append(acc[0:MLA_V] * (1.0 / acc[MLA_V:MLA_V + 1]))
    out = jnp.concatenate(outs, axis=0).T
    gate = gate_ref[...].astype(F32)
    o_ref[...] = (out * gate * _sigmoid(gate)).astype(o_ref.dtype)


def _attn(q, k, vt, o_mla, bsz, seq, tq, nh):
    n = bsz * seq
    nq = seq // tq
    return pl.pallas_call(
        functools.partial(_attn_kernel, tq, nh),
        grid=(bsz, MLA_HEADS // nh, nq),
        in_specs=[pl.BlockSpec((tq, nh * HEAD_PAD), lambda b, h, i: (b * nq + i, h)),
                  pl.BlockSpec((seq, nh * HEAD_PAD), lambda b, h, i: (b, h)),
                  pl.BlockSpec((1, nh, nq, HEAD_PAD, tq), lambda b, h, i: (b, h, 0, 0, 0)),
                  pl.BlockSpec((tq, nh * MLA_V), lambda b, h, i: (b * nq + i, h))],
        out_specs=pl.BlockSpec((tq, nh * MLA_V), lambda b, h, i: (b * nq + i, h)),
        out_shape=jax.ShapeDtypeStruct((n, MLA_WIDTH), BF16),
        scratch_shapes=[pltpu.VMEM((nh, 8, tq), F32), pltpu.VMEM((nh, V_ROWS, tq), F32)],
        compiler_params=pltpu.CompilerParams(dimension_semantics=("arbitrary",) * 3, vmem_limit_bytes=VMEM_LIMIT),
        name="attn",
    )(q, k, vt, o_mla)


def _merge_kernel(ya_ref, yb_ref, yc_ref, mg_ref, x_ref, wb_ref, wo_ref, pn_ref, o_ref):
    ys = (ya_ref[...], yb_ref[...], yc_ref[...])
    halves = []
    for c0 in range(0, D_MODEL, PROJ_CHUNK):
        acc = None
        for nb in range(N_BRANCHES):
            gates = _sigmoid(mg_ref[:, nb * D_MODEL + c0:nb * D_MODEL + c0 + PROJ_CHUNK].astype(F32))
            term = gates * _dot(ys[nb], wb_ref[nb, :, c0:c0 + PROJ_CHUNK])
            acc = term if acc is None else acc + term
        halves.append(acc.astype(BF16))
    mixed = _dot(jnp.concatenate(halves, axis=1), wo_ref[...])
    ms = jnp.mean(mixed * mixed, axis=-1, keepdims=True)
    o_ref[...] = x_ref[...] + mixed * lax.rsqrt(ms + NORM_EPS) * pn_ref[...]


def _merge(ya, yb, yc, o_mg, x2, w_branch, w_out, post_norm, tm):
    n = x2.shape[0]
    rowspec = lambda c: pl.BlockSpec((tm, c), lambda i: (i, 0))
    return pl.pallas_call(
        _merge_kernel,
        grid=(n // tm,),
        in_specs=[rowspec(BRANCH_WIDTH), rowspec(BRANCH_WIDTH), rowspec(BRANCH_WIDTH), rowspec(MG_COLS),
                  rowspec(D_MODEL), _const_spec(w_branch.shape), _const_spec(w_out.shape), _const_spec((1, D_MODEL))],
        out_specs=rowspec(D_MODEL),
        out_shape=jax.ShapeDtypeStruct((n, D_MODEL), F32),
        compiler_params=pltpu.CompilerParams(dimension_semantics=("arbitrary",), vmem_limit_bytes=VMEM_LIMIT),
        name="merge",
    )(ya, yb, yc, o_mg, x2, w_branch, w_out, post_norm)


def _row(v):
    return v.reshape(1, -1).astype(F32)


def _pad_rows(w, rows, at):
    out = jnp.zeros((rows, w.shape[1]), w.dtype)
    return out.at[at:at + w.shape[0]].set(w)


IN_SEGS = (RWKV_SHIFT_COLS, RWKV_WIDTH, GMLP_WIDTH, GMLP_WIDTH, GMLP_WIDTH,
           MLA_Q_LORA, MLA_KV_LORA, MLA_ROPE, MLA_WIDTH, N_BRANCHES * D_MODEL)
W_ROWS = 128


def _w_layout_kernel(has_vres, *refs):
    w_ref = refs[0]
    vd_ref = refs[1] if has_vres else None
    o_ref = refs[-1]
    o = [int(c) for c in np.cumsum((0,) + IN_SEGS)]
    cols = lambda c0, c1: w_ref[:, c0:c1]
    seg = lambda i: cols(o[i], o[i + 1])
    rows = w_ref.shape[0]
    zeros = lambda c: jnp.zeros((rows, c), F32)
    half = MLA_ROPE // 2
    k_rope = seg(7)
    misc_a = jnp.concatenate([vd_ref[...], zeros(MISC - RWKV_VRES_LORA)], axis=1) if has_vres else zeros(MISC)
    misc_b = jnp.concatenate([zeros(ROPE_LANE0), k_rope, k_rope[:, half:], k_rope[:, :half]], axis=1)
    col = 0
    for piece in (cols(o[0], o[2]), misc_a, cols(o[2], o[5]), seg(8), seg(5), seg(6), misc_b, seg(9)):
        o_ref[:, col:col + piece.shape[1]] = piece.astype(BF16)
        col += piece.shape[1]


def _w_layout(l, w_in, v_down):
    has_vres = l > 0
    blocks = D_MODEL // W_ROWS
    args = [w_in.reshape(-1, w_in.shape[2])] + ([v_down.reshape(-1, RWKV_VRES_LORA)] if has_vres else [])
    specs = [pl.BlockSpec((W_ROWS, w_in.shape[2]), lambda i: (l * blocks + i, 0))]
    if has_vres:
        specs.append(pl.BlockSpec((W_ROWS, RWKV_VRES_LORA), lambda i: ((l - 1) * blocks + i, 0)))
    return pl.pallas_call(
        functools.partial(_w_layout_kernel, has_vres),
        grid=(blocks,),
        in_specs=specs,
        out_specs=pl.BlockSpec((W_ROWS, ALL_COLS), lambda i: (i, 0)),
        out_shape=jax.ShapeDtypeStruct((D_MODEL, ALL_COLS), BF16),
        compiler_params=pltpu.CompilerParams(dimension_semantics=("arbitrary",), vmem_limit_bytes=VMEM_LIMIT),
        name="w_layout",
    )(*args)


def _layer_weights(l, w_in, rwkv_v_down, mla_w_uq, mla_w_ukv):
    w_all = _w_layout(l, w_in, rwkv_v_down)
    qd = MLA_NOPE + MLA_ROPE
    wq = mla_w_uq[l].reshape(MLA_Q_LORA, MLA_HEADS, qd)
    wq = _with_swapped_rope(jnp.pad(wq, ((0, 0), (0, 0), (0, HEAD_PAD - qd)))).reshape(MLA_Q_LORA, MLA_HEADS * HEAD_PAD)
    wkv = mla_w_ukv[l].reshape(MLA_KV_LORA, MLA_HEADS, MLA_NOPE + MLA_V)
    wk = jnp.pad(wkv[:, :, :MLA_NOPE], ((0, 0), (0, 0), (0, HEAD_PAD - MLA_NOPE)))
    wv = jnp.pad(wkv[:, :, MLA_NOPE:], ((0, 0), (0, 0), (0, HEAD_PAD - MLA_V)))
    wkv_all = jnp.concatenate([wk.reshape(MLA_KV_LORA, -1), wv.reshape(MLA_KV_LORA, -1)], axis=1)
    return w_all, wq.astype(BF16), wkv_all.astype(BF16)


def _constants(tc):
    head = np.arange(LANES) // RWKV_HEAD_DIM
    hsum = (head[:, None] == head[None, :]).astype(np.float32)
    t = np.arange(tc)
    same = (t[:, None] // SUB) == (t[None, :] // SUB)
    cum = np.concatenate([same & (t[:, None] >= t[None, :]), same], axis=0).astype(np.float32)
    return {"hsum": jnp.asarray(hsum, BF16), "cum": jnp.asarray(cum, BF16)}


def kernel(x, positions, w_in, pre_norm, post_norm, rwkv_mu, rwkv_w0, rwkv_w_up, rwkv_a0, rwkv_a_up, rwkv_k_k, rwkv_k_a, rwkv_r_k, rwkv_ln_w, rwkv_ln_b, rwkv_v0, rwkv_v_down, rwkv_v_up, gmlp_ln_w, gmlp_ln_b, gmlp_w_s, gmlp_b_s, mla_q_norm, mla_w_uq, mla_kv_norm, mla_w_ukv, w_branch, w_out):
    bsz, seq, _ = x.shape
    n = bsz * seq
    depth = w_in.shape[0]
    tc, tm, tq = min(TIME_TILE, seq), min(ROW_TILE, n), min(ATTN_TILE, seq)
    consts = _constants(tc)
    inv_freq = 1.0 / (ROPE_THETA ** (jnp.arange(0, MLA_ROPE, 2, dtype=F32) / MLA_ROPE))
    rope_tab = _rope_table(positions, inv_freq)
    spread = _rope_spread()

    x2 = x.reshape(n, D_MODEL)
    v_first = None
    for l in range(depth):
        w_all, wq, wkv = _layer_weights(l, w_in, rwkv_v_down, mla_w_uq, mla_w_ukv)
        ws = gmlp_w_s[l].reshape(GMLP_GROUPS // 2, 2, GMLP_CHUNK, GMLP_CHUNK)
        gprm = {"ln_w": _row(gmlp_ln_w[l]), "ln_b": _row(gmlp_ln_b[l]),
                "w_s": jnp.concatenate([ws[:, 0], ws[:, 1]], axis=-1),
                "bias": jnp.repeat(gmlp_b_s[l].T, GMLP_WIDTH // GMLP_GROUPS, axis=1)}
        rprm = {"mu": _row(rwkv_mu[l]), "w0": _row(rwkv_w0[l]), "a0": _row(rwkv_a0[l]),
                "w_up": _pad_rows(rwkv_w_up[l], LANES, 0).astype(BF16),
                "a_up": _pad_rows(rwkv_a_up[l], LANES, RWKV_DECAY_LORA).astype(BF16),
                "k_k": _row(rwkv_k_k[l]), "k_a": _row(rwkv_k_a[l]), "r_k": _row(rwkv_r_k[l]),
                "ln_w": _row(rwkv_ln_w[l]), "ln_b": _row(rwkv_ln_b[l]), "hsum": consts["hsum"], "cum": consts["cum"]}
        if l > 0:
            rprm["v0"] = _row(rwkv_v0[l - 1])
            rprm["v_up"] = _pad_rows(rwkv_v_up[l - 1], MISC, 0).astype(BF16)
        prepared, y_b, o_mla, o_mg = _in_proj(x2, _row(pre_norm[l]), w_all, v_first, rprm, gprm, seq, tc)
        if l == 0:
            v_first = prepared[-1]
        y_a = _rwkv(prepared, rprm, bsz, seq, tc)

        mprm = {"spread": spread, "q_norm": _row(mla_q_norm[l]), "w_uq": wq,
                "kv_norm": _row(mla_kv_norm[l]), "w_ukv": wkv}
        q, k, vt = _mla_prep(o_mla, rope_tab, mprm, bsz, seq, max(tq, min(tm, seq)), tq)
        y_c = _attn(q, k, vt, o_mla, bsz, seq, tq, ATTN_HEADS_PER_STEP)

        x2 = _merge(y_a, y_b, y_c, o_mg, x2, w_branch[l].astype(BF16), w_out[l].astype(BF16),
                    _row(post_norm[l]), tm)
    return x2.reshape(bsz, seq, D_MODEL)
```
